```python
import jax, jax.numpy as jnp
from jax import lax
import numpy as np

D_MODEL = 1024
BATCH = 16
SEQ = 2048
DEPTH = 1

CTX_LEN = 256
GRID_W = 64
POOL_WIDTH = D_MODEL
POOL_GROUPS = 4
POOL_GROUP_DIM = POOL_WIDTH // POOL_GROUPS
POOL_WINDOWS = (2, 4, 8, 16)
N_HEADS = 16
QK_NOPE = 128
QK_ROPE = 64
V_DIM = 128
Q_LORA = 256
KV_LORA = 128
MLA_WIDTH = N_HEADS * V_DIM
ROPE_THETA = 10000.0
Q_BLOCK = 128
NORM_EPS = 1e-6
IN_SPLITS = (POOL_WIDTH, POOL_WIDTH, Q_LORA, KV_LORA + QK_ROPE, MLA_WIDTH, D_MODEL, D_MODEL)
IN_WIDTH = sum(IN_SPLITS)

kernel_name = "hybrid_pool_mla_prefix_dit_block"


def rms_norm(x, g):
    xf = x.astype(jnp.float32)
    y = xf * lax.rsqrt(jnp.mean(xf * xf, axis=-1, keepdims=True) + NORM_EPS)
    return y.astype(x.dtype) * g


def ada_params(cond, w, b):
    m = jax.nn.silu(cond) @ w + b
    return jnp.split(m, 3, axis=-1)


def split_in(p):
    offsets = [int(o) for o in np.cumsum(IN_SPLITS)[:-1]]
    return jnp.split(p, offsets, axis=-1)


def axial_rope_tables(n_tokens):
    rows = n_tokens // GRID_W
    t_row = jnp.repeat(jnp.arange(rows, dtype=jnp.float32), GRID_W)
    t_col = jnp.tile(jnp.arange(GRID_W, dtype=jnp.float32), rows)
    half = QK_ROPE // 2
    inv = 1.0 / (ROPE_THETA ** (jnp.arange(0, half, 2, dtype=jnp.float32) / half))
    ang_r = t_row[:, None] * inv
    ang_c = t_col[:, None] * inv
    ang = jnp.concatenate([ang_r, ang_r, ang_c, ang_c], axis=-1)
    return jnp.cos(ang), jnp.sin(ang)


def apply_rope(x, cos, sin):
    xr1, xr2, xc1, xc2 = jnp.split(x, 4, axis=-1)
    rot = jnp.concatenate([-xr2, xr1, -xc2, xc1], axis=-1)
    return (x * cos + rot * sin).astype(x.dtype)


def pool_mix(u, w_pool, pool_scale):
    B, L, _ = u.shape
    ug = u.reshape(B, L, POOL_GROUPS, POOL_GROUP_DIM)
    cs = jnp.cumsum(ug.astype(jnp.float32), axis=1)
    cs = jnp.concatenate([jnp.zeros_like(cs[:, :1]), cs], axis=1)
    t = jnp.arange(L)
    pooled = []
    for gi, w in enumerate(POOL_WINDOWS):
        lo = jnp.clip(t - w // 2, 0, L)
        hi = jnp.clip(t + w // 2, 0, L)
        cnt = (hi - lo).astype(jnp.float32)
        csg = cs[:, :, gi]
        pooled.append((csg[:, hi] - csg[:, lo]) / cnt[:, None])
    pooled = jnp.stack(pooled, axis=2).astype(u.dtype)
    d = pooled - ug
    out = jnp.einsum('blgc,gcd->blgd', d, w_pool).reshape(B, L, POOL_WIDTH)
    return out * pool_scale


def mla_project(q_a, kv_a, q_norm_g, w_qb, kv_norm_g, w_kvb, rope):
    B, L, _ = q_a.shape
    q = (rms_norm(q_a, q_norm_g) @ w_qb).reshape(B, L, N_HEADS, QK_NOPE + QK_ROPE)
    q_nope, q_rope = q[..., :QK_NOPE], q[..., QK_NOPE:]
    c_kv, k_rope = kv_a[..., :KV_LORA], kv_a[..., KV_LORA:]
    kv = (rms_norm(c_kv, kv_norm_g) @ w_kvb).reshape(B, L, N_HEADS, QK_NOPE + V_DIM)
    k_nope, v = kv[..., :QK_NOPE], kv[..., QK_NOPE:]
    if rope is not None:
        cos, sin = rope
        q_rope = apply_rope(q_rope, cos[:, None, :], sin[:, None, :])
        k_rope = apply_rope(k_rope, cos, sin)
    return q_nope, q_rope, k_nope, k_rope, v


def block_attention(q_nope, q_rope, k_nope, k_rope, v):
    B, L, H, _ = q_nope.shape
    nb = L // Q_BLOCK
    scale = (QK_NOPE + QK_ROPE) ** -0.5

    def to_blocks(q):
        return q.reshape(B, nb, Q_BLOCK, H, q.shape[-1]).transpose(1, 0, 2, 3, 4)

    def one_block(qs):
        qn, qr = qs
        s = jnp.einsum('bqhd,bkhd->bhqk', qn, k_nope) + jnp.einsum('bqhr,bkr->bhqk', qr, k_rope)
        p = jax.nn.softmax(s.astype(jnp.float32) * scale, axis=-1).astype(v.dtype)
        return jnp.einsum('bhqk,bkhd->bqhd', p, v)

    o = lax.map(one_block, (to_blocks(q_nope), to_blocks(q_rope)))
    return o.transpose(1, 0, 2, 3, 4).reshape(B, L, H * V_DIM)


def merge_branches(pool_out, pool_gate, attn_out, mla_gate, m_pool, m_mla, b_gate, w_proj_pool, w_proj_mla, w_out):
    y_pool = (pool_out * jax.nn.silu(pool_gate)) @ w_proj_pool
    y_mla = (attn_out * jax.nn.silu(mla_gate)) @ w_proj_mla
    g_pool = jax.nn.sigmoid(m_pool + b_gate[:D_MODEL])
    g_mla = jax.nn.sigmoid(m_mla + b_gate[D_MODEL:])
    return (g_pool * y_pool + g_mla * y_mla) @ w_out


def setup_inputs(seed: int = 0) -> dict:
    key = jax.random.key(seed)
    ks = jax.random.split(key, 20)
    f32 = jnp.float32
    n = lambda k, shape, s: jax.random.normal(k, shape, f32) * s
    return {
        "x": n(ks[0], (BATCH, SEQ, D_MODEL), 1.0),
        "c": n(ks[1], (BATCH, D_MODEL), 1.0),
        "ctx": n(ks[2], (BATCH, CTX_LEN, D_MODEL), 1.0),
        "c_ctx": n(ks[3], (D_MODEL,), 1.0),
        "ada_w": n(ks[4], (DEPTH, D_MODEL, 3 * D_MODEL), D_MODEL ** -0.5),
        "ada_b": n(ks[5], (DEPTH, 3 * D_MODEL), 0.02),
        "norm_g": 1.0 + n(ks[6], (DEPTH, D_MODEL), 0.05),
        "w_in": n(ks[7], (DEPTH, D_MODEL, IN_WIDTH), D_MODEL ** -0.5),
        "b_gate": n(ks[8], (DEPTH, 2 * D_MODEL), 0.02),
        "w_pool": n(ks[9], (DEPTH, POOL_GROUPS, POOL_GROUP_DIM, POOL_GROUP_DIM), POOL_GROUP_DIM ** -0.5),
        "pool_scale": 1.0 + n(ks[10], (DEPTH, POOL_WIDTH), 0.1),
        "q_norm_g": 1.0 + n(ks[11], (DEPTH, Q_LORA), 0.05),
        "w_qb": n(ks[12], (DEPTH, Q_LORA, N_HEADS * (QK_NOPE + QK_ROPE)), Q_LORA ** -0.5),
        "kv_norm_g": 1.0 + n(ks[13], (DEPTH, KV_LORA), 0.05),
        "w_kvb": n(ks[14], (DEPTH, KV_LORA, N_HEADS * (QK_NOPE + V_DIM)), KV_LORA ** -0.5),
        "w_proj_pool": n(ks[15], (DEPTH, POOL_WIDTH, D_MODEL), POOL_WIDTH ** -0.5),
        "w_proj_mla": n(ks[16], (DEPTH, MLA_WIDTH, D_MODEL), MLA_WIDTH ** -0.5),
        "w_out": n(ks[17], (DEPTH, D_MODEL, D_MODEL), D_MODEL ** -0.5),
        "final_g": 1.0 + n(ks[18], (D_MODEL,), 0.05),
    }


def reference(x, c, ctx, c_ctx, ada_w, ada_b, norm_g, w_in, b_gate, w_pool, pool_scale, q_norm_g, w_qb,
              kv_norm_g, w_kvb, w_proj_pool, w_proj_mla, w_out, final_g):
    rope = axial_rope_tables(x.shape[1])
    for l in range(DEPTH):
        last = l == DEPTH - 1
        shift, scale, gate = ada_params(c, ada_w[l], ada_b[l])
        shift_c, scale_c, gate_c = ada_params(c_ctx, ada_w[l], ada_b[l])
        hx = rms_norm(x, norm_g[l]) * (1.0 + scale[:, None]) + shift[:, None]
        hc = rms_norm(ctx, norm_g[l]) * (1.0 + scale_c) + shift_c
        ux, gpx, qax, kvax, gmx, mpx, mmx = split_in(hx @ w_in[l])
        uc, gpc, qac, kvac, gmc, mpc, mmc = split_in(hc @ w_in[l])
        qn_x, qr_x, kn_x, kr_x, v_x = mla_project(qax, kvax, q_norm_g[l], w_qb[l], kv_norm_g[l], w_kvb[l], rope)
        qn_c, qr_c, kn_c, kr_c, v_c = mla_project(qac, kvac, q_norm_g[l], w_qb[l], kv_norm_g[l], w_kvb[l], None)
        attn_x = block_attention(qn_x, qr_x,
                                 jnp.concatenate([kn_x, kn_c], axis=1),
                                 jnp.concatenate([kr_x, kr_c], axis=1),
                                 jnp.concatenate([v_x, v_c], axis=1))
        pool_x = pool_mix(ux, w_pool[l], pool_scale[l])
        if not last:
            attn_c = block_attention(qn_c, qr_c, kn_c, kr_c, v_c)
            pool_c = pool_mix(uc, w_pool[l], pool_scale[l])
            ctx = ctx + gate_c * merge_branches(pool_c, gpc, attn_c, gmc, mpc, mmc, b_gate[l],
                                                w_proj_pool[l], w_proj_mla[l], w_out[l])
        x = x + gate[:, None] * merge_branches(pool_x, gpx, attn_x, gmx, mpx, mmx, b_gate[l],
                                               w_proj_pool[l], w_proj_mla[l], w_out[l])
    return rms_norm(x, final_g)
```

```python
import functools
import math

import jax
import jax.numpy as jnp
from jax import lax
from jax.experimental import pallas as pl
from jax.experimental.pallas import tpu as pltpu

GRID_W = 64
N_HEADS = 16
QK_NOPE = 128
QK_ROPE = 64
V_DIM = 128
POOL_WINDOWS = (2, 4, 8, 16)
ROPE_THETA = 10000.0
NORM_EPS = 1e-6

LANES = 128
SUBLANES = 8
MXU_DIM = 256
VMEM_LIMIT_BYTES = 56 * 1024 * 1024

TOKEN_TILE = 256
POOL_HALO = max(POOL_WINDOWS) // 2

F32 = jnp.float32
BF16 = jnp.bfloat16


def _sigmoid(v):
    return 1.0 / (1.0 + jnp.exp(-v))


def _silu(v):
    return v * _sigmoid(v)


def _rms(v):
    return v * lax.rsqrt(jnp.mean(v * v, axis=-1, keepdims=True) + NORM_EPS)


def _dot(a, b):
    return jnp.dot(a, b, preferred_element_type=F32)


def _rope(v, tab_ref):
    return (v * tab_ref[0]
            + pltpu.roll(v, LANES - QK_ROPE // 4, 1) * tab_ref[1]
            + pltpu.roll(v, QK_ROPE // 4, 1) * tab_ref[2])


def _ada_kernel(cond_ref, w_ref, b_ref, o_ref):
    s = _silu(cond_ref[...]).astype(BF16)
    o_ref[...] = _dot(s, w_ref[...].astype(BF16)) + b_ref[...]


def _ada_params(cond, w, b):
    rows, d = cond.shape
    n = w.shape[1]
    bn = d
    return pl.pallas_call(
        _ada_kernel,
        grid=(n // bn,),
        in_specs=[pl.BlockSpec((rows, d), lambda j: (0, 0)),
                  pl.BlockSpec((d, bn), lambda j: (0, j)),
                  pl.BlockSpec((1, bn), lambda j: (0, j))],
        out_specs=pl.BlockSpec((rows, bn), lambda j: (0, j)),
        out_shape=jax.ShapeDtypeStruct((rows, n), F32),
        name="ada_params",
    )(cond, w, b.reshape(1, n))


def _qabs_kernel(wq_ref, wkt_ref, o_ref):
    o_ref[...] = jnp.dot(wq_ref[...], wkt_ref[0], precision=lax.Precision.HIGHEST,
                         preferred_element_type=F32)


def _fold_key_proj(wq_nope, wk_t):
    q_lora = wq_nope.shape[0]
    kv_lora = wk_t.shape[2]
    return pl.pallas_call(
        _qabs_kernel,
        grid=(N_HEADS,),
        in_specs=[pl.BlockSpec((q_lora, QK_NOPE), lambda h: (0, h)),
                  pl.BlockSpec((1, QK_NOPE, kv_lora), lambda h: (h, 0, 0))],
        out_specs=pl.BlockSpec((q_lora, kv_lora), lambda h: (0, h)),
        out_shape=jax.ShapeDtypeStruct((q_lora, N_HEADS * kv_lora), F32),
        name="fold_key_proj",
    )(wq_nope, wk_t)


def _kv_kernel(x_ref, ctx_ref, modb_ref, modc_ref, ng_ref, wkv_ref, kvg_ref, tab_ref,
               ksh_ref, ckvt_ref, *, n_lat_tiles):
    is_lat = pl.program_id(1) < n_lat_tiles
    xt = jnp.where(is_lat, x_ref[0], ctx_ref[0])
    mod = jnp.where(is_lat, modb_ref[0], modc_ref[0])
    h = (_rms(xt) * ng_ref[...] * (1.0 + mod[1:2]) + mod[0:1]).astype(BF16)
    kv = _dot(h, wkv_ref[...])
    ckvn = _rms(kv[:, :LANES]) * kvg_ref[...]
    kr = _rope(kv[:, LANES:], tab_ref)
    ksh_ref[0] = jnp.concatenate([ckvn, kr], axis=1).astype(BF16)
    ckvt_ref[0] = ckvn.T.astype(BF16)


def _shared_kv(x, ctx, mod, norm_g, wkv, kv_norm_g, tabs):
    bsz, seq, d = x.shape
    ctx_len = ctx.shape[1]
    t = TOKEN_TILE
    assert ctx_len == t and seq % t == 0
    n_lat = seq // t
    keys = seq + ctx_len
    ctx_row = mod.shape[0] - 1
    return pl.pallas_call(
        functools.partial(_kv_kernel, n_lat_tiles=n_lat),
        grid=(bsz, n_lat + 1),
        in_specs=[pl.BlockSpec((1, t, d), lambda b, j: (b, jnp.minimum(j, n_lat - 1), 0)),
                  pl.BlockSpec((1, t, d), lambda b, j: (b, 0, 0)),
                  pl.BlockSpec((1, 3, d), lambda b, j: (b, 0, 0)),
                  pl.BlockSpec((1, 3, d), lambda b, j: (ctx_row, 0, 0)),
                  pl.BlockSpec((1, d), lambda b, j: (0, 0)),
                  pl.BlockSpec((d, MXU_DIM), lambda b, j: (0, 0)),
                  pl.BlockSpec((1, LANES), lambda b, j: (0, 0)),
                  pl.BlockSpec((3, t, LANES), lambda b, j: (0, j, 0))],
        out_specs=[pl.BlockSpec((1, t, MXU_DIM), lambda b, j: (b, j, 0)),
                   pl.BlockSpec((1, LANES, t), lambda b, j: (b, 0, j))],
        out_shape=[jax.ShapeDtypeStruct((bsz, keys, MXU_DIM), BF16),
                   jax.ShapeDtypeStruct((bsz, LANES, keys), BF16)],
        name="shared_kv",
    )(x, ctx, mod, mod, norm_g, wkv, kv_norm_g, tabs)


def _block_kernel(x_ref, xp_ref, xn_ref, mod_ref, tab_ref, ksh_ref, ckvt_ref,
                  ng_ref, qg_ref, bg_ref, ps_ref, fg_ref,
                  wu_ref, wgp_ref, wqa_ref, wqc_ref, wgm_ref, wmp_ref, wmm_ref,
                  wpool_ref, wv_ref, wpa_ref, wpb_ref, wout_ref,
                  o_ref, h_scr, qn_scr, ymla_scr, uext_scr, *, seq_len, exp2_scale):
    t = TOKEN_TILE
    i = pl.program_id(1)
    mod = mod_ref[0]
    shift, scale1, gate = mod[0:1], 1.0 + mod[1:2], mod[2:3]
    ng = ng_ref[...]

    def mod_norm(v):
        return _rms(v) * ng * scale1 + shift

    x = x_ref[0]
    h_main = mod_norm(x)
    h_halo = mod_norm(jnp.concatenate([xp_ref[0], xn_ref[0]], axis=0))
    hb = h_main.astype(BF16)
    h_scr[...] = hb

    qa = _dot(hb, wqa_ref[...])
    qn_scr[...] = (_rms(qa) * qg_ref[...]).astype(BF16)
    ymla_scr[...] = jnp.zeros_like(ymla_scr)

    def head(hh):
        qh = _dot(qn_scr[...], wqc_ref[hh])
        qc = jnp.concatenate([qh[:, :LANES], _rope(qh[:, LANES:], tab_ref)], axis=1).astype(BF16)
        st = lax.dot_general(ksh_ref[0], qc, (((1,), (1,)), ((), ())),
                             preferred_element_type=F32)
        m = jnp.max(st, axis=0, keepdims=True)
        p = jnp.exp2((st - m) * exp2_scale)
        l = jnp.sum(p, axis=0, keepdims=True)
        ot = _dot(ckvt_ref[0], p.astype(BF16))
        return (ot * (1.0 / l)).T

    def head_pair(j, carry):
        o2 = jnp.concatenate([head(2 * j), head(2 * j + 1)], axis=1).astype(BF16)
        ov = _dot(o2, wv_ref[j])
        gm = _dot(h_scr[...], wgm_ref[j])
        ymla_scr[...] += _dot((ov * _silu(gm)).astype(BF16), wpb_ref[j])
        return carry

    lax.fori_loop(0, N_HEADS // 2, head_pair, 0)

    h_all = jnp.concatenate([h_main, h_halo], axis=0).astype(BF16)
    u_all = _dot(h_all, wu_ref[...])
    has_prev = (i > 0).astype(F32)
    has_next = (i < pl.num_programs(1) - 1).astype(F32)
    uext_scr[0:POOL_HALO] = u_all[t:t + POOL_HALO] * has_prev
    uext_scr[POOL_HALO:POOL_HALO + t] = u_all[0:t]
    uext_scr[POOL_HALO + t:2 * POOL_HALO + t] = u_all[t + POOL_HALO:t + 2 * POOL_HALO] * has_next
    gw = uext_scr.shape[1] // len(POOL_WINDOWS)
    pos = i * t + lax.broadcasted_iota(jnp.int32, (t, gw), 0)
    mixed = []
    for g, w in enumerate(POOL_WINDOWS):
        hw = w // 2
        cols = slice(g * gw, (g + 1) * gw)
        acc = uext_scr[pl.ds(POOL_HALO - hw, t), cols]
        for k in range(1, w):
            acc = acc + uext_scr[pl.ds(POOL_HALO - hw + k, t), cols]
        cnt = (jnp.minimum(pos + hw, seq_len) - jnp.maximum(pos - hw, 0)).astype(F32)
        d = acc / cnt - uext_scr[pl.ds(POOL_HALO, t), cols]
        mixed.append(_dot(d.astype(BF16), wpool_ref[g]))
    pool_out = jnp.concatenate(mixed, axis=1) * ps_ref[...]
    gp = _dot(hb, wgp_ref[...])
    y_pool = _dot((pool_out * _silu(gp)).astype(BF16), wpa_ref[...])

    g_pool = _sigmoid(_dot(hb, wmp_ref[...]) + bg_ref[0:1])
    g_mla = _sigmoid(_dot(hb, wmm_ref[...]) + bg_ref[1:2])
    z = (g_pool * y_pool + g_mla * ymla_scr[...]).astype(BF16)
    out = x + gate * _dot(z, wout_ref[...])
    o_ref[0] = _rms(out) * fg_ref[...]


def _const_spec(shape):
    zeros = (0,) * len(shape)
    return pl.BlockSpec(shape, lambda b, i: zeros, pipeline_mode=pl.Buffered(1))


def _fused_block(x, mod, tabs, ksh, ckvt, vecs, weights):
    bsz, seq, d = x.shape
    t = TOKEN_TILE
    keys = ksh.shape[1]
    halo_blocks = t // POOL_HALO
    last_halo = seq // POOL_HALO - 1
    in_specs = [
        pl.BlockSpec((1, t, d), lambda b, i: (b, i, 0)),
        pl.BlockSpec((1, POOL_HALO, d), lambda b, i: (b, jnp.maximum(i * halo_blocks - 1, 0), 0)),
        pl.BlockSpec((1, POOL_HALO, d), lambda b, i: (b, jnp.minimum((i + 1) * halo_blocks, last_halo), 0)),
        pl.BlockSpec((1, 3, d), lambda b, i: (b, 0, 0)),
        pl.BlockSpec((3, t, LANES), lambda b, i: (0, i, 0)),
        pl.BlockSpec((1, keys, MXU_DIM), lambda b, i: (b, 0, 0)),
        pl.BlockSpec((1, LANES, keys), lambda b, i: (b, 0, 0)),
    ]
    in_specs += [_const_spec(v.shape) for v in vecs]
    in_specs += [_const_spec(w.shape) for w in weights]
    exp2_scale = float((QK_NOPE + QK_ROPE) ** -0.5 * math.log2(math.e))
    return pl.pallas_call(
        functools.partial(_block_kernel, seq_len=seq, exp2_scale=exp2_scale),
        grid=(bsz, seq // t),
        in_specs=in_specs,
        out_specs=pl.BlockSpec((1, t, d), lambda b, i: (b, i, 0)),
        out_shape=jax.ShapeDtypeStruct((bsz, seq, d), F32),
        scratch_shapes=[pltpu.VMEM((t, d), BF16),
                        pltpu.VMEM((t, weights[2].shape[1]), BF16),
                        pltpu.VMEM((t, d), F32),
                        pltpu.VMEM((t + 2 * POOL_HALO, d), F32)],
        compiler_params=pltpu.CompilerParams(
            dimension_semantics=("arbitrary", "arbitrary"),
            vmem_limit_bytes=VMEM_LIMIT_BYTES),
        name="fused_block",
    )(x, x, x, mod, tabs, ksh, ckvt, *vecs, *weights)


def _rope_tables(n_tokens, n_plain):
    t = jnp.arange(n_tokens)
    t_row = (t // GRID_W).astype(F32)
    t_col = (t % GRID_W).astype(F32)
    half = QK_ROPE // 2
    inv = 1.0 / (ROPE_THETA ** (jnp.arange(0, half, 2, dtype=F32) / half))
    ang_r = t_row[:, None] * inv
    ang_c = t_col[:, None] * inv
    ang = jnp.concatenate([ang_r, ang_r, ang_c, ang_c], axis=-1)
    cos, sin = jnp.cos(ang), jnp.sin(ang)
    quarter = QK_ROPE // 4
    first = (jnp.arange(QK_ROPE) % (2 * quarter)) < quarter
    sin_a = jnp.where(first, -sin, 0.0)
    sin_b = jnp.where(first, 0.0, sin)
    tabs = jnp.stack([cos, sin_a, sin_b])
    plain = jnp.stack([jnp.ones((n_plain, QK_ROPE), F32),
                       jnp.zeros((n_plain, QK_ROPE), F32),
                       jnp.zeros((n_plain, QK_ROPE), F32)])
    tabs = jnp.concatenate([tabs, plain], axis=1)
    return jnp.pad(tabs, ((0, 0), (0, 0), (0, LANES - QK_ROPE)))


def kernel(x, c, ctx, c_ctx, ada_w, ada_b, norm_g, w_in, b_gate, w_pool, pool_scale, q_norm_g, w_qb,
           kv_norm_g, w_kvb, w_proj_pool, w_proj_mla, w_out, final_g):
    assert ada_w.shape[0] == 1, "single-layer block"
    bsz, seq, d = x.shape
    ctx_len = ctx.shape[1]
    q_lora = q_norm_g.shape[1]
    kv_lora = kv_norm_g.shape[1]
    pool_w = pool_scale.shape[1]
    mla_w = w_proj_mla.shape[1]
    assert kv_lora == LANES and QK_NOPE == LANES and V_DIM == LANES and 2 * QK_ROPE == LANES

    cond = jnp.concatenate([c, c_ctx[None, :]], axis=0)
    mod = _ada_params(cond, ada_w[0], ada_b[0]).reshape(bsz + 1, 3, d)

    splits = (pool_w, pool_w, q_lora, kv_lora + QK_ROPE, mla_w, d, d)
    offs = [0]
    for s in splits:
        offs.append(offs[-1] + s)
    w_in0 = w_in[0]
    wu, wgp, wqa, wkva, wgm, wmp, wmm = [w_in0[:, offs[k]:offs[k + 1]] for k in range(7)]
    wkv_pad = jnp.pad(wkva, ((0, 0), (0, MXU_DIM - wkva.shape[1]))).astype(BF16)

    wqb3 = w_qb[0].reshape(q_lora, N_HEADS, QK_NOPE + QK_ROPE)
    wkvb3 = w_kvb[0].reshape(kv_lora, N_HEADS, QK_NOPE + V_DIM)
    wk_t = jnp.transpose(wkvb3[:, :, :QK_NOPE], (1, 2, 0))
    wv = jnp.transpose(wkvb3[:, :, QK_NOPE:], (1, 0, 2))
    wq_abs = _fold_key_proj(wqb3[:, :, :QK_NOPE].reshape(q_lora, N_HEADS * QK_NOPE), wk_t)
    wq_cat = jnp.concatenate([wq_abs.reshape(q_lora, N_HEADS, kv_lora),
                              wqb3[:, :, QK_NOPE:],
                              jnp.zeros((q_lora, N_HEADS, MXU_DIM - kv_lora - QK_ROPE), F32)], axis=2)
    wq_cat = jnp.transpose(wq_cat, (1, 0, 2)).astype(BF16)

    wv2 = wv.reshape(N_HEADS // 2, 2, kv_lora, V_DIM)
    zblk = jnp.zeros_like(wv2[:, 0])
    wv_bd = jnp.concatenate([jnp.concatenate([wv2[:, 0], zblk], axis=-1),
                             jnp.concatenate([zblk, wv2[:, 1]], axis=-1)], axis=1).astype(BF16)
    wgm3 = jnp.transpose(wgm.reshape(d, N_HEADS // 2, 2 * V_DIM), (1, 0, 2)).astype(BF16)
    wpb3 = w_proj_mla[0].reshape(N_HEADS // 2, 2 * V_DIM, d).astype(BF16)

    tabs = _rope_tables(seq, ctx_len)
    ksh, ckvt = _shared_kv(x, ctx, mod, norm_g, wkv_pad, kv_norm_g, tabs)

    vecs = (norm_g, q_norm_g, b_gate[0].reshape(2, d), pool_scale, final_g.reshape(1, d))
    weights = (wu.astype(BF16), wgp.astype(BF16), wqa.astype(BF16), wq_cat, wgm3, wmp.astype(BF16),
               wmm.astype(BF16), w_pool[0].astype(BF16), wv_bd, w_proj_pool[0].astype(BF16), wpb3,
               w_out[0].astype(BF16))
    return _fused_block(x, mod, tabs, ksh, ckvt, vecs, weights)
```

```python
import functools
import math

import jax
import jax.numpy as jnp
from jax import lax
from jax.experimental import pallas as pl
from jax.experimental.pallas import tpu as pltpu

GRID_W = 64
N_HEADS = 16
QK_NOPE = 128
QK_ROPE = 64
V_DIM = 128
POOL_WINDOWS = (2, 4, 8, 16)
ROPE_THETA = 10000.0
NORM_EPS = 1e-6

LANES = 128
SUBLANES = 8
MXU_DIM = 256
VMEM_LIMIT_BYTES = 56 * 1024 * 1024

TOKEN_TILE = 256
POOL_HALO = max(POOL_WINDOWS) // 2

F32 = jnp.float32
BF16 = jnp.bfloat16


def _sigmoid(v):
    return 1.0 / (1.0 + jnp.exp(-v))


def _silu(v):
    return v * _sigmoid(v)


def _rms(v):
    return v * lax.rsqrt(jnp.mean(v * v, axis=-1, keepdims=True) + NORM_EPS)


def _dot(a, b):
    return jnp.dot(a, b, preferred_element_type=F32)


def _rope(v, tab_ref):
    return (v * tab_ref[0]
            + pltpu.roll(v, LANES - QK_ROPE // 4, 1) * tab_ref[1]
            + pltpu.roll(v, QK_ROPE // 4, 1) * tab_ref[2])


def _ada_kernel(cond_ref, w_ref, b_ref, o_ref):
    s = _silu(cond_ref[...]).astype(BF16)
    o_ref[...] = _dot(s, w_ref[...].astype(BF16)) + b_ref[...]


def _ada_params(cond, w, b):
    rows, d = cond.shape
    n = w.shape[1]
    bn = d
    return pl.pallas_call(
        _ada_kernel,
        grid=(n // bn,),
        in_specs=[pl.BlockSpec((rows, d), lambda j: (0, 0)),
                  pl.BlockSpec((d, bn), lambda j: (0, j)),
                  pl.BlockSpec((1, bn), lambda j: (0, j))],
        out_specs=pl.BlockSpec((rows, bn), lambda j: (0, j)),
        out_shape=jax.ShapeDtypeStruct((rows, n), F32),
        name="ada_params",
    )(cond, w, b.reshape(1, n))


def _qabs_kernel(wq_ref, wkt_ref, o_ref):
    o_ref[...] = jnp.dot(wq_ref[...], wkt_ref[0], precision=lax.Precision.HIGHEST,
                         preferred_element_type=F32)


def _fold_key_proj(wq_nope, wk_t):
    q_lora = wq_nope.shape[0]
    kv_lora = wk_t.shape[2]
    return pl.pallas_call(
        _qabs_kernel,
        grid=(N_HEADS,),
        in_specs=[pl.BlockSpec((q_lora, QK_NOPE), lambda h: (0, h)),
                  pl.BlockSpec((1, QK_NOPE, kv_lora), lambda h: (h, 0, 0))],
        out_specs=pl.BlockSpec((q_lora, kv_lora), lambda h: (0, h)),
        out_shape=jax.ShapeDtypeStruct((q_lora, N_HEADS * kv_lora), F32),
        name="fold_key_proj",
    )(wq_nope, wk_t)


def _kv_kernel(x_ref, ctx_ref, modb_ref, modc_ref, ng_ref, wkv_ref, kvg_ref, tab_ref,
               ksh_ref, ckvt_ref, *, n_lat_tiles):
    is_lat = pl.program_id(1) < n_lat_tiles
    xt = jnp.where(is_lat, x_ref[0], ctx_ref[0])
    mod = jnp.where(is_lat, modb_ref[0], modc_ref[0])
    h = (_rms(xt) * ng_ref[...] * (1.0 + mod[1:2]) + mod[0:1]).astype(BF16)
    kv = _dot(h, wkv_ref[...])
    ckvn = _rms(kv[:, :LANES]) * kvg_ref[...]
    kr = _rope(kv[:, LANES:], tab_ref)
    ksh_ref[0] = jnp.concatenate([ckvn, kr], axis=1).astype(BF16)
    ckvt_ref[0] = ckvn.T.astype(BF16)


def _shared_kv(x, ctx, mod, norm_g, wkv, kv_norm_g, tabs):
    bsz, seq, d = x.shape
    ctx_len = ctx.shape[1]
    t = TOKEN_TILE
    assert ctx_len == t and seq % t == 0
    n_lat = seq // t
    keys = seq + ctx_len
    ctx_row = mod.shape[0] - 1
    return pl.pallas_call(
        functools.partial(_kv_kernel, n_lat_tiles=n_lat),
        grid=(bsz, n_lat + 1),
        in_specs=[pl.BlockSpec((1, t, d), lambda b, j: (b, jnp.minimum(j, n_lat - 1), 0)),
                  pl.BlockSpec((1, t, d), lambda b, j: (b, 0, 0)),
                  pl.BlockSpec((1, 3, d), lambda b, j: (b, 0, 0)),
                  pl.BlockSpec((1, 3, d), lambda b, j: (ctx_row, 0, 0)),
                  pl.BlockSpec((1, d), lambda b, j: (0, 0)),
                  pl.BlockSpec((d, MXU_DIM), lambda b, j: (0, 0)),
                  pl.BlockSpec((1, LANES), lambda b, j: (0, 0)),
                  pl.BlockSpec((3, t, LANES), lambda b, j: (0, j, 0))],
        out_specs=[pl.BlockSpec((1, t, MXU_DIM), lambda b, j: (b, j, 0)),
                   pl.BlockSpec((1, LANES, t), lambda b, j: (b, 0, j))],
        out_shape=[jax.ShapeDtypeStruct((bsz, keys, MXU_DIM), BF16),
                   jax.ShapeDtypeStruct((bsz, LANES, keys), BF16)],
        name="shared_kv",
    )(x, ctx, mod, mod, norm_g, wkv, kv_norm_g, tabs)


def _block_kernel(x_ref, xp_ref, xn_ref, mod_ref, tab_ref, ksh_ref, ckvt_ref,
                  ng_ref, qg_ref, bg_ref, ps_ref, fg_ref,
                  wu_ref, wgp_ref, wqa_ref, wqc_ref, wgm_ref, wmp_ref, wmm_ref,
                  wpool_ref, wv_ref, wpa_ref, wpb_ref, wout_ref,
                  o_ref, h_scr, qc_scr, attn_scr, uext_scr, s0_scr, s1_scr, *, seq_len, exp2_scale):
    t = TOKEN_TILE
    i = pl.program_id(1)
    mod = mod_ref[0]
    shift, scale1, gate = mod[0:1], 1.0 + mod[1:2], mod[2:3]
    ng = ng_ref[...]

    def mod_norm(v):
        return _rms(v) * ng * scale1 + shift

    x = x_ref[0]
    h_main = mod_norm(x)
    h_halo = mod_norm(jnp.concatenate([xp_ref[0], xn_ref[0]], axis=0))
    hb = h_main.astype(BF16)
    h_scr[...] = hb

    qa = _dot(hb, wqa_ref[...])
    q_all = _dot((_rms(qa) * qg_ref[...]).astype(BF16), wqc_ref[...]) * exp2_scale
    for hh in range(N_HEADS):
        c0 = hh * MXU_DIM
        qc_scr[hh] = jnp.concatenate([q_all[:, c0:c0 + LANES],
                                      _rope(q_all[:, c0 + LANES:c0 + MXU_DIM], tab_ref)],
                                     axis=1).astype(BF16)

    keys = ksh_ref.shape[1]
    kc = MXU_DIM
    n_chunks = keys // kc

    def score_chunk(c, hh, s_scr, m8):
        rows = slice(c * kc, (c + 1) * kc)
        sc = lax.dot_general(ksh_ref[0, rows, :], qc_scr[hh], (((1,), (1,)), ((), ())),
                             preferred_element_type=F32)
        s_scr[rows, :] = sc
        return jnp.maximum(m8, jnp.max(sc.reshape(kc // SUBLANES, SUBLANES, t), axis=0))

    def value_chunk(c, s_scr, m, l8, ot):
        rows = slice(c * kc, (c + 1) * kc)
        p = jnp.exp2(s_scr[rows, :] - m)
        l8 = l8 + jnp.sum(p.reshape(kc // SUBLANES, SUBLANES, t), axis=0)
        return l8, ot + _dot(ckvt_ref[0, :, rows], p.astype(BF16))

    def scores_only(hh, s_scr):
        m8 = jnp.full((SUBLANES, t), -jnp.inf, F32)
        for c in range(n_chunks):
            m8 = score_chunk(c, hh, s_scr, m8)
        return jnp.max(m8, axis=0, keepdims=True)

    def head_step(hh_next, s_next, s_cur, m_cur):
        m8 = jnp.full((SUBLANES, t), -jnp.inf, F32)
        l8 = jnp.zeros((SUBLANES, t), F32)
        ot = jnp.zeros((LANES, t), F32)
        for c in range(n_chunks):
            m8 = score_chunk(c, hh_next, s_next, m8)
            l8, ot = value_chunk(c, s_cur, m_cur, l8, ot)
        l = jnp.sum(l8, axis=0, keepdims=True)
        return jnp.max(m8, axis=0, keepdims=True), (ot * (1.0 / l)).T

    def head_pair(j, m_even):
        m_odd, o_even = head_step(2 * j + 1, s1_scr, s0_scr, m_even)
        m_next, o_odd = head_step(jnp.minimum(2 * j + 2, N_HEADS - 1), s0_scr, s1_scr, m_odd)
        attn_scr[j] = jnp.concatenate([o_even, o_odd], axis=1).astype(BF16)
        return m_next

    lax.fori_loop(0, N_HEADS // 2, head_pair, scores_only(0, s0_scr))

    attn = jnp.concatenate([_dot(attn_scr[j], wv_ref[j]) for j in range(N_HEADS // 2)], axis=1)
    gm = _dot(h_scr[...], wgm_ref[...])
    y_mla = _dot((attn * _silu(gm)).astype(BF16), wpb_ref[...])

    h_all = jnp.concatenate([h_main, h_halo], axis=0).astype(BF16)
    u_all = _dot(h_all, wu_ref[...])
    has_prev = (i > 0).astype(F32)
    has_next = (i < pl.num_programs(1) - 1).astype(F32)
    uext_scr[0:POOL_HALO] = u_all[t:t + POOL_HALO] * has_prev
    uext_scr[POOL_HALO:POOL_HALO + t] = u_all[0:t]
    uext_scr[POOL_HALO + t:2 * POOL_HALO + t] = u_all[t + POOL_HALO:t + 2 * POOL_HALO] * has_next
    gw = uext_scr.shape[1] // len(POOL_WINDOWS)
    pos = i * t + lax.broadcasted_iota(jnp.int32, (t, gw), 0)
    mixed = []
    for g, w in enumerate(POOL_WINDOWS):
        hw = w // 2
        cols = slice(g * gw, (g + 1) * gw)
        acc = uext_scr[pl.ds(POOL_HALO - hw, t), cols]
        for k in range(1, w):
            acc = acc + uext_scr[pl.ds(POOL_HALO - hw + k, t), cols]
        cnt = (jnp.minimum(pos + hw, seq_len) - jnp.maximum(pos - hw, 0)).astype(F32)
        d = acc / cnt - uext_scr[pl.ds(POOL_HALO, t), cols]
        mixed.append(_dot(d.astype(BF16), wpool_ref[g]))
    pool_out = jnp.concatenate(mixed, axis=1) * ps_ref[...]
    gp = _dot(h_scr[...], wgp_ref[...])
    y_pool = _dot((pool_out * _silu(gp)).astype(BF16), wpa_ref[...])

    g_pool = _sigmoid(_dot(h_scr[...], wmp_ref[...]) + bg_ref[0:1])
    g_mla = _sigmoid(_dot(h_scr[...], wmm_ref[...]) + bg_ref[1:2])
    z = (g_pool * y_pool + g_mla * y_mla).astype(BF16)
    out = x + gate * _dot(z, wout_ref[...])
    o_ref[0] = _rms(out) * fg_ref[...]


def _const_spec(shape):
    zeros = (0,) * len(shape)
    return pl.BlockSpec(shape, lambda b, i: zeros, pipeline_mode=pl.Buffered(1))


def _fused_block(x, mod, tabs, ksh, ckvt, vecs, weights):
    bsz, seq, d = x.shape
    t = TOKEN_TILE
    keys = ksh.shape[1]
    halo_blocks = t // POOL_HALO
    last_halo = seq // POOL_HALO - 1
    in_specs = [
        pl.BlockSpec((1, t, d), lambda b, i: (b, i, 0)),
        pl.BlockSpec((1, POOL_HALO, d), lambda b, i: (b, jnp.maximum(i * halo_blocks - 1, 0), 0)),
        pl.BlockSpec((1, POOL_HALO, d), lambda b, i: (b, jnp.minimum((i + 1) * halo_blocks, last_halo), 0)),
        pl.BlockSpec((1, 3, d), lambda b, i: (b, 0, 0)),
        pl.BlockSpec((3, t, LANES), lambda b, i: (0, i, 0)),
        pl.BlockSpec((1, keys, MXU_DIM), lambda b, i: (b, 0, 0)),
        pl.BlockSpec((1, LANES, keys), lambda b, i: (b, 0, 0)),
    ]
    in_specs += [_const_spec(v.shape) for v in vecs]
    in_specs += [_const_spec(w.shape) for w in weights]
    exp2_scale = float((QK_NOPE + QK_ROPE) ** -0.5 * math.log2(math.e))
    return pl.pallas_call(
        functools.partial(_block_kernel, seq_len=seq, exp2_scale=exp2_scale),
        grid=(bsz, seq // t),
        in_specs=in_specs,
        out_specs=pl.BlockSpec((1, t, d), lambda b, i: (b, i, 0)),
        out_shape=jax.ShapeDtypeStruct((bsz, seq, d), F32),
        scratch_shapes=[pltpu.VMEM((t, d), BF16),
                        pltpu.VMEM((N_HEADS, t, MXU_DIM), BF16),
                        pltpu.VMEM((N_HEADS // 2, t, 2 * V_DIM), BF16),
                        pltpu.VMEM((t + 2 * POOL_HALO, d), F32),
                        pltpu.VMEM((keys, t), F32),
                        pltpu.VMEM((keys, t), F32)],
        compiler_params=pltpu.CompilerParams(
            dimension_semantics=("arbitrary", "arbitrary"),
            vmem_limit_bytes=VMEM_LIMIT_BYTES),
        name="fused_block",
    )(x, x, x, mod, tabs, ksh, ckvt, *vecs, *weights)


def _rope_tables(n_tokens, n_plain):
    t = jnp.arange(n_tokens)
    t_row = (t // GRID_W).astype(F32)
    t_col = (t % GRID_W).astype(F32)
    half = QK_ROPE // 2
    inv = 1.0 / (ROPE_THETA ** (jnp.arange(0, half, 2, dtype=F32) / half))
    ang_r = t_row[:, None] * inv
    ang_c = t_col[:, None] * inv
    ang = jnp.concatenate([ang_r, ang_r, ang_c, ang_c], axis=-1)
    cos, sin = jnp.cos(ang), jnp.sin(ang)
    quarter = QK_ROPE // 4
    first = (jnp.arange(QK_ROPE) % (2 * quarter)) < quarter
    sin_a = jnp.where(first, -sin, 0.0)
    sin_b = jnp.where(first, 0.0, sin)
    tabs = jnp.stack([cos, sin_a, sin_b])
    plain = jnp.stack([jnp.ones((n_plain, QK_ROPE), F32),
                       jnp.zeros((n_plain, QK_ROPE), F32),
                       jnp.zeros((n_plain, QK_ROPE), F32)])
    tabs = jnp.concatenate([tabs, plain], axis=1)
    return jnp.pad(tabs, ((0, 0), (0, 0), (0, LANES - QK_ROPE)))


def kernel(x, c, ctx, c_ctx, ada_w, ada_b, norm_g, w_in, b_gate, w_pool, pool_scale, q_norm_g, w_qb,
           kv_norm_g, w_kvb, w_proj_pool, w_proj_mla, w_out, final_g):
    assert ada_w.shape[0] == 1, "single-layer block"
    bsz, seq, d = x.shape
    ctx_len = ctx.shape[1]
    q_lora = q_norm_g.shape[1]
    kv_lora = kv_norm_g.shape[1]
    pool_w = pool_scale.shape[1]
    mla_w = w_proj_mla.shape[1]
    assert kv_lora == LANES and QK_NOPE == LANES and V_DIM == LANES and 2 * QK_ROPE == LANES

    cond = jnp.concatenate([c, c_ctx[None, :]], axis=0)
    mod = _ada_params(cond, ada_w[0], ada_b[0]).reshape(bsz + 1, 3, d)

    splits = (pool_w, pool_w, q_lora, kv_lora + QK_ROPE, mla_w, d, d)
    offs = [0]
    for s in splits:
        offs.append(offs[-1] + s)
    w_in0 = w_in[0]
    wu, wgp, wqa, wkva, wgm, wmp, wmm = [w_in0[:, offs[k]:offs[k + 1]] for k in range(7)]
    wkv_pad = jnp.pad(wkva, ((0, 0), (0, MXU_DIM - wkva.shape[1]))).astype(BF16)

    wqb3 = w_qb[0].reshape(q_lora, N_HEADS, QK_NOPE + QK_ROPE)
    wkvb3 = w_kvb[0].reshape(kv_lora, N_HEADS, QK_NOPE + V_DIM)
    wk_t = jnp.transpose(wkvb3[:, :, :QK_NOPE], (1, 2, 0))
    wv = jnp.transpose(wkvb3[:, :, QK_NOPE:], (1, 0, 2))
    wq_abs = _fold_key_proj(wqb3[:, :, :QK_NOPE].reshape(q_lora, N_HEADS * QK_NOPE), wk_t)
    wq_cat = jnp.concatenate([wq_abs.reshape(q_lora, N_HEADS, kv_lora),
                              wqb3[:, :, QK_NOPE:],
                              jnp.zeros((q_lora, N_HEADS, MXU_DIM - kv_lora - QK_ROPE), F32)], axis=2)
    wq_cat = wq_cat.reshape(q_lora, N_HEADS * MXU_DIM).astype(BF16)

    wv2 = wv.reshape(N_HEADS // 2, 2, kv_lora, V_DIM)
    zblk = jnp.zeros_like(wv2[:, 0])
    wv_bd = jnp.concatenate([jnp.concatenate([wv2[:, 0], zblk], axis=-1),
                             jnp.concatenate([zblk, wv2[:, 1]], axis=-1)], axis=1).astype(BF16)

    tabs = _rope_tables(seq, ctx_len)
    ksh, ckvt = _shared_kv(x, ctx, mod, norm_g, wkv_pad, kv_norm_g, tabs)

    vecs = (norm_g, q_norm_g, b_gate[0].reshape(2, d), pool_scale, final_g.reshape(1, d))
    weights = (wu.astype(BF16), wgp.astype(BF16), wqa.astype(BF16), wq_cat, wgm.astype(BF16),
               wmp.astype(BF16), wmm.astype(BF16), w_pool[0].astype(BF16), wv_bd,
               w_proj_pool[0].astype(BF16), w_proj_mla[0].astype(BF16), w_out[0].astype(BF16))
    return _fused_block(x, mod, tabs, ksh, ckvt, vecs, weights)
```

```python
import functools
import math

import jax
import jax.numpy as jnp
from jax import lax
from jax.experimental import pallas as pl
from jax.experimental.pallas import tpu as pltpu

GRID_W = 64
N_HEADS = 16
QK_NOPE = 128
QK_ROPE = 64
V_DIM = 128
POOL_WINDOWS = (2, 4, 8, 16)
ROPE_THETA = 10000.0
NORM_EPS = 1e-6

LANES = 128
SUBLANES = 8
MXU_DIM = 256
VMEM_LIMIT_BYTES = 56 * 1024 * 1024

TOKEN_TILE = 256
POOL_HALO = max(POOL_WINDOWS) // 2
SCORE_LEAD = 2
HEADS_PER_TRIP = 4

F32 = jnp.float32
BF16 = jnp.bfloat16


def _sigmoid(v):
    return 0.5 + 0.5 * jnp.tanh(0.5 * v)


def _silu(v):
    hv = 0.5 * v
    return hv + hv * jnp.tanh(hv)


def _rms(v):
    return v * lax.rsqrt(jnp.mean(v * v, axis=-1, keepdims=True) + NORM_EPS)


def _dot(a, b):
    return jnp.dot(a, b, preferred_element_type=F32)


def _rope(v, tab_ref):
    return (v * tab_ref[0]
            + pltpu.roll(v, LANES - QK_ROPE // 4, 1) * tab_ref[1]
            + pltpu.roll(v, QK_ROPE // 4, 1) * tab_ref[2])


def _ada_kernel(cond_ref, w_ref, b_ref, o_ref):
    s = _silu(cond_ref[...]).astype(BF16)
    o_ref[...] = _dot(s, w_ref[...].astype(BF16)) + b_ref[...]


def _ada_params(cond, w, b):
    rows, d = cond.shape
    n = w.shape[1]
    bn = d
    return pl.pallas_call(
        _ada_kernel,
        grid=(n // bn,),
        in_specs=[pl.BlockSpec((rows, d), lambda j: (0, 0)),
                  pl.BlockSpec((d, bn), lambda j: (0, j)),
                  pl.BlockSpec((1, bn), lambda j: (0, j))],
        out_specs=pl.BlockSpec((rows, bn), lambda j: (0, j)),
        out_shape=jax.ShapeDtypeStruct((rows, n), F32),
        name="ada_params",
    )(cond, w, b.reshape(1, n))


def _qabs_kernel(wq_ref, wkt_ref, o_ref):
    o_ref[...] = jnp.dot(wq_ref[...], wkt_ref[0], precision=lax.Precision.HIGHEST,
                         preferred_element_type=F32)


def _fold_key_proj(wq_nope, wk_t):
    q_lora = wq_nope.shape[0]
    kv_lora = wk_t.shape[2]
    return pl.pallas_call(
        _qabs_kernel,
        grid=(N_HEADS,),
        in_specs=[pl.BlockSpec((q_lora, QK_NOPE), lambda h: (0, h)),
                  pl.BlockSpec((1, QK_NOPE, kv_lora), lambda h: (h, 0, 0))],
        out_specs=pl.BlockSpec((q_lora, kv_lora), lambda h: (0, h)),
        out_shape=jax.ShapeDtypeStruct((q_lora, N_HEADS * kv_lora), F32),
        name="fold_key_proj",
    )(wq_nope, wk_t)


def _kv_kernel(x_ref, ctx_ref, modb_ref, modc_ref, ng_ref, wkv_ref, kvg_ref, tab_ref,
               ksh_ref, ckvt_ref, *, n_lat_tiles):
    is_lat = pl.program_id(1) < n_lat_tiles
    xt = jnp.where(is_lat, x_ref[0], ctx_ref[0])
    mod = jnp.where(is_lat, modb_ref[0], modc_ref[0])
    h = (_rms(xt) * ng_ref[...] * (1.0 + mod[1:2]) + mod[0:1]).astype(BF16)
    kv = _dot(h, wkv_ref[...])
    ckvn = _rms(kv[:, :LANES]) * kvg_ref[...]
    kr = _rope(kv[:, LANES:], tab_ref)
    ksh_ref[0] = jnp.concatenate([ckvn, kr], axis=1).astype(BF16)
    ckvt_ref[0] = ckvn.T.astype(BF16)


def _shared_kv(x, ctx, mod, norm_g, wkv, kv_norm_g, tabs):
    bsz, seq, d = x.shape
    ctx_len = ctx.shape[1]
    t = TOKEN_TILE
    assert ctx_len == t and seq % t == 0
    n_lat = seq // t
    keys = seq + ctx_len
    ctx_row = mod.shape[0] - 1
    return pl.pallas_call(
        functools.partial(_kv_kernel, n_lat_tiles=n_lat),
        grid=(bsz, n_lat + 1),
        in_specs=[pl.BlockSpec((1, t, d), lambda b, j: (b, jnp.minimum(j, n_lat - 1), 0)),
                  pl.BlockSpec((1, t, d), lambda b, j: (b, 0, 0)),
                  pl.BlockSpec((1, 3, d), lambda b, j: (b, 0, 0)),
                  pl.BlockSpec((1, 3, d), lambda b, j: (ctx_row, 0, 0)),
                  pl.BlockSpec((1, d), lambda b, j: (0, 0)),
                  pl.BlockSpec((d, MXU_DIM), lambda b, j: (0, 0)),
                  pl.BlockSpec((1, LANES), lambda b, j: (0, 0)),
                  pl.BlockSpec((3, t, LANES), lambda b, j: (0, j, 0))],
        out_specs=[pl.BlockSpec((1, t, MXU_DIM), lambda b, j: (b, j, 0)),
                   pl.BlockSpec((1, LANES, t), lambda b, j: (b, 0, j))],
        out_shape=[jax.ShapeDtypeStruct((bsz, keys, MXU_DIM), BF16),
                   jax.ShapeDtypeStruct((bsz, LANES, keys), BF16)],
        name="shared_kv",
    )(x, ctx, mod, mod, norm_g, wkv, kv_norm_g, tabs)


def _block_kernel(x_ref, xp_ref, xn_ref, mod_ref, tab_ref, ksh_ref, ckvt_ref,
                  ng_ref, qg_ref, bg_ref, ps_ref, fg_ref,
                  wu_ref, wgp_ref, wqa_ref, wqc_ref, wgm_ref, wmp_ref, wmm_ref,
                  wpool_ref, wv_ref, wpa_ref, wpb_ref, wout_ref,
                  o_ref, h_scr, qc_scr, ot_scr, l_scr, uext_scr, lvl_scr, s0_scr, s1_scr, *,
                  seq_len, exp2_scale):
    t = TOKEN_TILE
    i = pl.program_id(1)
    mod = mod_ref[0]
    shift, scale1, gate = mod[0:1], 1.0 + mod[1:2], mod[2:3]
    ng = ng_ref[...]

    def mod_norm(v):
        return _rms(v) * ng * scale1 + shift

    x = x_ref[0]
    h_main = mod_norm(x)
    h_halo = mod_norm(jnp.concatenate([xp_ref[0], xn_ref[0]], axis=0))
    hb = h_main.astype(BF16)
    h_scr[...] = hb

    qa = _dot(hb, wqa_ref[...])
    q_all = _dot((_rms(qa) * qg_ref[...]).astype(BF16), wqc_ref[...]) * exp2_scale
    for hh in range(N_HEADS):
        c0 = hh * MXU_DIM
        qc_scr[hh] = jnp.concatenate([q_all[:, c0:c0 + LANES],
                                      _rope(q_all[:, c0 + LANES:c0 + MXU_DIM], tab_ref)],
                                     axis=1).astype(BF16)

    keys = ksh_ref.shape[1]
    kc = MXU_DIM
    n_chunks = keys // kc

    def score_chunk(c, hh, s_scr, m8):
        rows = slice(c * kc, (c + 1) * kc)
        sc = lax.dot_general(ksh_ref[0, rows, :], qc_scr[hh], (((1,), (1,)), ((), ())),
                             preferred_element_type=F32)
        s_scr[rows, :] = sc
        return jnp.maximum(m8, jnp.max(sc.reshape(kc // SUBLANES, SUBLANES, t), axis=0))

    def value_chunk(c, s_scr, m, l8, ot):
        rows = slice(c * kc, (c + 1) * kc)
        p = jnp.exp2(s_scr[rows, :] - m)
        l8 = l8 + jnp.sum(p.reshape(kc // SUBLANES, SUBLANES, t), axis=0)
        return l8, ot + _dot(ckvt_ref[0, :, rows], p.astype(BF16))

    neg_inf8 = jnp.full((SUBLANES, t), -jnp.inf, F32)

    def head_step(hh, h1, h2, s_cur, s_oth, m_cur, m8_1):
        m8_2 = neg_inf8
        l8 = jnp.zeros((SUBLANES, t), F32)
        ot = jnp.zeros((LANES, t), F32)
        for c in range(n_chunks):
            ahead = c + SCORE_LEAD
            if ahead < n_chunks:
                m8_1 = score_chunk(ahead, h1, s_oth, m8_1)
            else:
                m8_2 = score_chunk(ahead - n_chunks, h2, s_cur, m8_2)
            l8, ot = value_chunk(c, s_cur, m_cur, l8, ot)
        ot_scr[hh] = ot
        l_scr[hh] = l8
        return jnp.max(m8_1, axis=0, keepdims=True), m8_2

    def head_trip(j, carry):
        m_cur, m8_next = carry
        bufs = (s0_scr, s1_scr)
        for k in range(HEADS_PER_TRIP):
            hh = HEADS_PER_TRIP * j + k
            h1 = jnp.minimum(hh + 1, N_HEADS - 1)
            h2 = jnp.minimum(hh + 2, N_HEADS - 1)
            m_cur, m8_next = head_step(hh, h1, h2, bufs[k % 2], bufs[(k + 1) % 2], m_cur, m8_next)
        return m_cur, m8_next

    m8_0 = neg_inf8
    for c in range(n_chunks):
        m8_0 = score_chunk(c, 0, s0_scr, m8_0)
    m8_1 = neg_inf8
    for c in range(SCORE_LEAD):
        m8_1 = score_chunk(c, 1, s1_scr, m8_1)
    lax.fori_loop(0, N_HEADS // HEADS_PER_TRIP, head_trip, (jnp.max(m8_0, axis=0, keepdims=True), m8_1))

    def latent_out(hh):
        inv_l = 1.0 / jnp.sum(l_scr[hh], axis=0, keepdims=True)
        return (ot_scr[hh] * inv_l).T

    attn = jnp.concatenate(
        [_dot(jnp.concatenate([latent_out(2 * j), latent_out(2 * j + 1)], axis=1).astype(BF16), wv_ref[j])
         for j in range(N_HEADS // 2)], axis=1)
    gm = _dot(h_scr[...], wgm_ref[...])
    y_mla = _dot((attn * _silu(gm)).astype(BF16), wpb_ref[...])

    h_all = jnp.concatenate([h_main, h_halo], axis=0).astype(BF16)
    has_prev = (i > 0).astype(F32)
    has_next = (i < pl.num_programs(1) - 1).astype(F32)
    n_ext = t + 2 * POOL_HALO
    gw = uext_scr.shape[1] // len(POOL_WINDOWS)
    uext_scr[n_ext:n_ext + SUBLANES] = jnp.zeros((SUBLANES, uext_scr.shape[1]), F32)
    for lv in range(lvl_scr.shape[0]):
        lvl_scr[lv, n_ext:n_ext + SUBLANES] = jnp.zeros((SUBLANES, gw), F32)
    pos = i * t + lax.broadcasted_iota(jnp.int32, (t, LANES), 0)

    def group_inputs(g):
        cols = slice(g * gw, (g + 1) * gw)
        return _dot(h_all, wu_ref[:, cols]), _dot(h_scr[...], wgp_ref[:, cols])

    y_pool = jnp.zeros((t, wpa_ref.shape[1]), F32)
    nxt = group_inputs(0)
    for g, w in enumerate(POOL_WINDOWS):
        hw = w // 2
        cols = slice(g * gw, (g + 1) * gw)
        u_g, gp_g = nxt
        if g + 1 < len(POOL_WINDOWS):
            nxt = group_inputs(g + 1)
        uext_scr[0:POOL_HALO, cols] = u_g[t:t + POOL_HALO] * has_prev
        uext_scr[POOL_HALO:POOL_HALO + t, cols] = u_g[0:t]
        uext_scr[POOL_HALO + t:n_ext, cols] = u_g[t + POOL_HALO:n_ext] * has_next

        def rows_of(k, n, lv=None):
            return uext_scr[pl.ds(k, n), cols] if lv is None else lvl_scr[lv, pl.ds(k, n)]
        lv, span = None, 1
        while span < hw:
            nlv = 0 if lv is None else lv + 1
            lvl_scr[nlv, 0:n_ext] = rows_of(0, n_ext, lv) + rows_of(span, n_ext, lv)
            lv, span = nlv, 2 * span
        acc = rows_of(POOL_HALO - hw, t, lv) + rows_of(POOL_HALO, t, lv)
        cnt = (jnp.minimum(pos + hw, seq_len) - jnp.maximum(pos - hw, 0)).astype(F32)
        inv_cnt = jnp.concatenate([1.0 / cnt] * (gw // LANES), axis=1)
        d = acc * inv_cnt - uext_scr[pl.ds(POOL_HALO, t), cols]
        mixed = _dot(d.astype(BF16), wpool_ref[g]) * ps_ref[:, cols]
        y_pool = y_pool + _dot((mixed * _silu(gp_g)).astype(BF16), wpa_ref[cols, :])

    g_pool = _sigmoid(_dot(h_scr[...], wmp_ref[...]) + bg_ref[0:1])
    g_mla = _sigmoid(_dot(h_scr[...], wmm_ref[...]) + bg_ref[1:2])
    z = (g_pool * y_pool + g_mla * y_mla).astype(BF16)
    out = x + gate * _dot(z, wout_ref[...])
    o_ref[0] = _rms(out) * fg_ref[...]


def _const_spec(shape):
    zeros = (0,) * len(shape)
    return pl.BlockSpec(shape, lambda b, i: zeros, pipeline_mode=pl.Buffered(1))


def _fused_block(x, mod, tabs, ksh, ckvt, vecs, weights):
    bsz, seq, d = x.shape
    t = TOKEN_TILE
    keys = ksh.shape[1]
    halo_blocks = t // POOL_HALO
    last_halo = seq // POOL_HALO - 1
    in_specs = [
        pl.BlockSpec((1, t, d), lambda b, i: (b, i, 0)),
        pl.BlockSpec((1, POOL_HALO, d), lambda b, i: (b, jnp.maximum(i * halo_blocks - 1, 0), 0)),
        pl.BlockSpec((1, POOL_HALO, d), lambda b, i: (b, jnp.minimum((i + 1) * halo_blocks, last_halo), 0)),
        pl.BlockSpec((1, 3, d), lambda b, i: (b, 0, 0)),
        pl.BlockSpec((3, t, LANES), lambda b, i: (0, i, 0)),
        pl.BlockSpec((1, keys, MXU_DIM), lambda b, i: (b, 0, 0)),
        pl.BlockSpec((1, LANES, keys), lambda b, i: (b, 0, 0)),
    ]
    in_specs += [_const_spec(v.shape) for v in vecs]
    in_specs += [_const_spec(w.shape) for w in weights]
    exp2_scale = float((QK_NOPE + QK_ROPE) ** -0.5 * math.log2(math.e))
    return pl.pallas_call(
        functools.partial(_block_kernel, seq_len=seq, exp2_scale=exp2_scale),
        grid=(bsz, seq // t),
        in_specs=in_specs,
        out_specs=pl.BlockSpec((1, t, d), lambda b, i: (b, i, 0)),
        out_shape=jax.ShapeDtypeStruct((bsz, seq, d), F32),
        scratch_shapes=[pltpu.VMEM((t, d), BF16),
                        pltpu.VMEM((N_HEADS, t, MXU_DIM), BF16),
                        pltpu.VMEM((N_HEADS, LANES, t), F32),
                        pltpu.VMEM((N_HEADS, SUBLANES, t), F32),
                        pltpu.VMEM((t + 2 * POOL_HALO + SUBLANES, d), F32),
                        pltpu.VMEM((len(POOL_WINDOWS) - 1, t + 2 * POOL_HALO + SUBLANES,
                                    d // len(POOL_WINDOWS)), F32),
                        pltpu.VMEM((keys, t), F32),
                        pltpu.VMEM((keys, t), F32)],
        compiler_params=pltpu.CompilerParams(
            dimension_semantics=("arbitrary", "arbitrary"),
            vmem_limit_bytes=VMEM_LIMIT_BYTES),
        name="fused_block",
    )(x, x, x, mod, tabs, ksh, ckvt, *vecs, *weights)


def _rope_tables(n_tokens, n_plain):
    t = jnp.arange(n_tokens)
    t_row = (t // GRID_W).astype(F32)
    t_col = (t % GRID_W).astype(F32)
    half = QK_ROPE // 2
    inv = 1.0 / (ROPE_THETA ** (jnp.arange(0, half, 2, dtype=F32) / half))
    ang_r = t_row[:, None] * inv
    ang_c = t_col[:, None] * inv
    ang = jnp.concatenate([ang_r, ang_r, ang_c, ang_c], axis=-1)
    cos, sin = jnp.cos(ang), jnp.sin(ang)
    quarter = QK_ROPE // 4
    first = (jnp.arange(QK_ROPE) % (2 * quarter)) < quarter
    sin_a = jnp.where(first, -sin, 0.0)
    sin_b = jnp.where(first, 0.0, sin)
    tabs = jnp.stack([cos, sin_a, sin_b])
    plain = jnp.stack([jnp.ones((n_plain, QK_ROPE), F32),
                       jnp.zeros((n_plain, QK_ROPE), F32),
                       jnp.zeros((n_plain, QK_ROPE), F32)])
    tabs = jnp.concatenate([tabs, plain], axis=1)
    return jnp.pad(tabs, ((0, 0), (0, 0), (0, LANES - QK_ROPE)))


def kernel(x, c, ctx, c_ctx, ada_w, ada_b, norm_g, w_in, b_gate, w_pool, pool_scale, q_norm_g, w_qb,
           kv_norm_g, w_kvb, w_proj_pool, w_proj_mla, w_out, final_g):
    assert ada_w.shape[0] == 1, "single-layer block"
    bsz, seq, d = x.shape
    ctx_len = ctx.shape[1]
    q_lora = q_norm_g.shape[1]
    kv_lora = kv_norm_g.shape[1]
    pool_w = pool_scale.shape[1]
    mla_w = w_proj_mla.shape[1]
    assert kv_lora == LANES and QK_NOPE == LANES and V_DIM == LANES and 2 * QK_ROPE == LANES

    cond = jnp.concatenate([c, c_ctx[None, :]], axis=0)
    mod = _ada_params(cond, ada_w[0], ada_b[0]).reshape(bsz + 1, 3, d)

    splits = (pool_w, pool_w, q_lora, kv_lora + QK_ROPE, mla_w, d, d)
    offs = [0]
    for s in splits:
        offs.append(offs[-1] + s)
    w_in0 = w_in[0]
    wu, wgp, wqa, wkva, wgm, wmp, wmm = [w_in0[:, offs[k]:offs[k + 1]] for k in range(7)]
    wkv_pad = jnp.pad(wkva, ((0, 0), (0, MXU_DIM - wkva.shape[1]))).astype(BF16)

    wqb3 = w_qb[0].reshape(q_lora, N_HEADS, QK_NOPE + QK_ROPE)
    wkvb3 = w_kvb[0].reshape(kv_lora, N_HEADS, QK_NOPE + V_DIM)
    wk_t = jnp.transpose(wkvb3[:, :, :QK_NOPE], (1, 2, 0))
    wv = jnp.transpose(wkvb3[:, :, QK_NOPE:], (1, 0, 2))
    wq_abs = _fold_key_proj(wqb3[:, :, :QK_NOPE].reshape(q_lora, N_HEADS * QK_NOPE), wk_t)
    wq_cat = jnp.concatenate([wq_abs.reshape(q_lora, N_HEADS, kv_lora),
                              wqb3[:, :, QK_NOPE:],
                              jnp.zeros((q_lora, N_HEADS, MXU_DIM - kv_lora - QK_ROPE), F32)], axis=2)
    wq_cat = wq_cat.reshape(q_lora, N_HEADS * MXU_DIM).astype(BF16)

    wv2 = wv.reshape(N_HEADS // 2, 2, kv_lora, V_DIM)
    zblk = jnp.zeros_like(wv2[:, 0])
    wv_bd = jnp.concatenate([jnp.concatenate([wv2[:, 0], zblk], axis=-1),
                             jnp.concatenate([zblk, wv2[:, 1]], axis=-1)], axis=1).astype(BF16)

    tabs = _rope_tables(seq, ctx_len)
    ksh, ckvt = _shared_kv(x, ctx, mod, norm_g, wkv_pad, kv_norm_g, tabs)

    vecs = (norm_g, q_norm_g, b_gate[0].reshape(2, d), pool_scale, final_g.reshape(1, d))
    weights = (wu.astype(BF16), wgp.astype(BF16), wqa.astype(BF16), wq_cat, wgm.astype(BF16),
               wmp.astype(BF16), wmm.astype(BF16), w_pool[0].astype(BF16), wv_bd,
               w_proj_pool[0].astype(BF16), w_proj_mla[0].astype(BF16), w_out[0].astype(BF16))
    return _fused_block(x, mod, tabs, ksh, ckvt, vecs, weights)
```

```python
import functools
import math

import jax
import jax.numpy as jnp
from jax import lax
from jax.experimental import pallas as pl
from jax.experimental.pallas import tpu as pltpu

GRID_W = 64
N_HEADS = 16
QK_NOPE = 128
QK_ROPE = 64
V_DIM = 128
POOL_WINDOWS = (2, 4, 8, 16)
ROPE_THETA = 10000.0
NORM_EPS = 1e-6

LANES = 128
SUBLANES = 8
MXU_DIM = 256
VMEM_LIMIT_BYTES = 56 * 1024 * 1024

TOKEN_TILE = 256
KV_TILE = 1024
KV_VMEM_LIMIT_BYTES = 40 * 1024 * 1024
POOL_HALO = max(POOL_WINDOWS) // 2
SCORE_LEAD = 2
HEADS_PER_TRIP = 8

F32 = jnp.float32
BF16 = jnp.bfloat16


def _sigmoid(v):
    return 0.5 + 0.5 * jnp.tanh(0.5 * v)


def _silu(v):
    hv = 0.5 * v
    return hv + hv * jnp.tanh(hv)


def _rms(v):
    return v * lax.rsqrt(jnp.mean(v * v, axis=-1, keepdims=True) + NORM_EPS)


def _dot(a, b):
    return jnp.dot(a, b, preferred_element_type=F32)


def _rope(v, tab_ref):
    return (v * tab_ref[0]
            + pltpu.roll(v, LANES - QK_ROPE // 4, 1) * tab_ref[1]
            + pltpu.roll(v, QK_ROPE // 4, 1) * tab_ref[2])


def _ada_kernel(cond_ref, w_ref, b_ref, o_ref):
    s = _silu(cond_ref[...]).astype(BF16)
    o_ref[...] = _dot(s, w_ref[...].astype(BF16)) + b_ref[...]


def _ada_params(cond, w, b):
    rows, d = cond.shape
    n = w.shape[1]
    bn = d
    return pl.pallas_call(
        _ada_kernel,
        grid=(n // bn,),
        in_specs=[pl.BlockSpec((rows, d), lambda j: (0, 0)),
                  pl.BlockSpec((d, bn), lambda j: (0, j)),
                  pl.BlockSpec((1, bn), lambda j: (0, j))],
        out_specs=pl.BlockSpec((rows, bn), lambda j: (0, j)),
        out_shape=jax.ShapeDtypeStruct((rows, n), F32),
        name="ada_params",
    )(cond, w, b.reshape(1, n))


def _qabs_kernel(wq_ref, wkt_ref, o_ref):
    o_ref[...] = jnp.dot(wq_ref[...], wkt_ref[0], precision=lax.Precision.HIGHEST,
                         preferred_element_type=F32)


def _fold_key_proj(wq_nope, wk_t):
    q_lora = wq_nope.shape[0]
    kv_lora = wk_t.shape[2]
    return pl.pallas_call(
        _qabs_kernel,
        grid=(N_HEADS,),
        in_specs=[pl.BlockSpec((q_lora, QK_NOPE), lambda h: (0, h)),
                  pl.BlockSpec((1, QK_NOPE, kv_lora), lambda h: (h, 0, 0))],
        out_specs=pl.BlockSpec((q_lora, kv_lora), lambda h: (0, h)),
        out_shape=jax.ShapeDtypeStruct((q_lora, N_HEADS * kv_lora), F32),
        name="fold_key_proj",
    )(wq_nope, wk_t)


def _kv_kernel(x_ref, mod_ref, ng_ref, wkv_ref, kvg_ref, tab_ref, ksh_ref, ckvt_ref):
    mod = mod_ref[0]
    h = (_rms(x_ref[0]) * ng_ref[...] * (1.0 + mod[1:2]) + mod[0:1]).astype(BF16)
    kv = _dot(h, wkv_ref[...])
    ckvn = _rms(kv[:, :LANES]) * kvg_ref[...]
    kr = _rope(kv[:, LANES:], tab_ref)
    kr = kr + pltpu.roll(kr, QK_ROPE, 1)
    ksh_ref[0] = jnp.concatenate([ckvn, kr], axis=1).astype(BF16)
    ckvt_ref[0] = ckvn.T.astype(BF16)


def _shared_kv(tokens, mod, mod_row, norm_g, wkv, kv_norm_g, tabs, tile):
    bsz, n, d = tokens.shape
    assert n % tile == 0
    mod_idx = (lambda b, j: (b, 0, 0)) if mod_row is None else (lambda b, j: (mod_row, 0, 0))
    return pl.pallas_call(
        _kv_kernel,
        grid=(bsz, n // tile),
        in_specs=[pl.BlockSpec((1, tile, d), lambda b, j: (b, j, 0)),
                  pl.BlockSpec((1, 3, d), mod_idx),
                  pl.BlockSpec((1, d), lambda b, j: (0, 0)),
                  pl.BlockSpec((d, MXU_DIM), lambda b, j: (0, 0)),
                  pl.BlockSpec((1, LANES), lambda b, j: (0, 0)),
                  pl.BlockSpec((3, tile, LANES), lambda b, j: (0, j, 0))],
        out_specs=[pl.BlockSpec((1, tile, MXU_DIM), lambda b, j: (b, j, 0)),
                   pl.BlockSpec((1, LANES, tile), lambda b, j: (b, 0, j))],
        out_shape=[jax.ShapeDtypeStruct((bsz, n, MXU_DIM), BF16),
                   jax.ShapeDtypeStruct((bsz, LANES, n), BF16)],
        compiler_params=pltpu.CompilerParams(vmem_limit_bytes=KV_VMEM_LIMIT_BYTES),
        name="shared_kv",
    )(tokens, mod, norm_g, wkv, kv_norm_g, tabs)


def _block_kernel(x_ref, xp_ref, xn_ref, mod_ref, tab_ref, ksh_ref, ckvt_ref, kshc_ref, ckvtc_ref,
                  ng_ref, qg_ref, bg_ref, ps_ref, fg_ref,
                  wu_ref, wgp_ref, wqa_ref, wqc_ref, wgm_ref, wmp_ref, wmm_ref,
                  wpool_ref, wv_ref, wpa_ref, wpb_ref, wout_ref,
                  o_ref, h_scr, qc_scr, ot_scr, l_scr, uext_scr, lvl_scr, s0_scr, s1_scr, *,
                  seq_len, exp2_scale):
    t = TOKEN_TILE
    i = pl.program_id(1)
    mod = mod_ref[0]
    shift, scale1, gate = mod[0:1], 1.0 + mod[1:2], mod[2:3]
    ng = ng_ref[...]

    def mod_norm(v):
        return _rms(v) * ng * scale1 + shift

    x = x_ref[0]
    h_main = mod_norm(x)
    h_halo = mod_norm(jnp.concatenate([xp_ref[0], xn_ref[0]], axis=0))
    hb = h_main.astype(BF16)
    h_scr[...] = hb

    qa = _dot(hb, wqa_ref[...])
    gate_mla = _silu(_dot(hb, wgm_ref[...]))
    q_all = _dot((_rms(qa) * qg_ref[...]).astype(BF16), wqc_ref[...]) * exp2_scale
    for hh in range(N_HEADS):
        c0 = hh * MXU_DIM
        qc_scr[hh] = jnp.concatenate([q_all[:, c0:c0 + LANES],
                                      q_all[:, c0 + LANES:c0 + MXU_DIM] * tab_ref[...]],
                                     axis=1).astype(BF16)

    keys = ksh_ref.shape[1] + kshc_ref.shape[1]
    n_lat_chunks = ksh_ref.shape[1] // MXU_DIM
    kc = MXU_DIM
    n_chunks = keys // kc

    def score_chunk(c, hh, s_scr, m8):
        rows = slice(c * kc, (c + 1) * kc)
        k_c = (ksh_ref[0, rows, :] if c < n_lat_chunks
               else kshc_ref[0, (c - n_lat_chunks) * kc:(c - n_lat_chunks + 1) * kc, :])
        sc = lax.dot_general(k_c, qc_scr[hh], (((1,), (1,)), ((), ())),
                             preferred_element_type=F32)
        s_scr[rows, :] = sc
        return jnp.maximum(m8, jnp.max(sc.reshape(kc // SUBLANES, SUBLANES, t), axis=0))

    def value_chunk(c, s_scr, m, l8, ot):
        rows = slice(c * kc, (c + 1) * kc)
        p = jnp.exp2(s_scr[rows, :] - m)
        l8 = l8 + jnp.sum(p.reshape(kc // SUBLANES, SUBLANES, t), axis=0)
        v_c = (ckvt_ref[0, :, rows] if c < n_lat_chunks
               else ckvtc_ref[0, :, (c - n_lat_chunks) * kc:(c - n_lat_chunks + 1) * kc])
        return l8, ot + _dot(v_c, p.astype(BF16))

    neg_inf8 = jnp.full((SUBLANES, t), -jnp.inf, F32)

    def head_step(hh, h1, h2, s_cur, s_oth, m_cur, m8_1):
        m8_2 = neg_inf8
        l8 = jnp.zeros((SUBLANES, t), F32)
        ot = jnp.zeros((LANES, t), F32)
        for c in range(n_chunks):
            ahead = c + SCORE_LEAD
            if ahead < n_chunks:
                m8_1 = score_chunk(ahead, h1, s_oth, m8_1)
            else:
                m8_2 = score_chunk(ahead - n_chunks, h2, s_cur, m8_2)
            l8, ot = value_chunk(c, s_cur, m_cur, l8, ot)
        ot_scr[hh] = ot
        l_scr[hh] = l8
        return jnp.max(m8_1, axis=0, keepdims=True), m8_2

    def head_trip(j, carry):
        m_cur, m8_next = carry
        bufs = (s0_scr, s1_scr)
        for k in range(HEADS_PER_TRIP):
            hh = HEADS_PER_TRIP * j + k
            h1 = jnp.minimum(hh + 1, N_HEADS - 1)
            h2 = jnp.minimum(hh + 2, N_HEADS - 1)
            m_cur, m8_next = head_step(hh, h1, h2, bufs[k % 2], bufs[(k + 1) % 2], m_cur, m8_next)
        return m_cur, m8_next

    m8_0, m8_1 = neg_inf8, neg_inf8
    for c in range(n_chunks):
        m8_0 = score_chunk(c, 0, s0_scr, m8_0)
    for c in range(SCORE_LEAD):
        m8_1 = score_chunk(c, 1, s1_scr, m8_1)
    lax.fori_loop(0, N_HEADS // HEADS_PER_TRIP, head_trip, (jnp.max(m8_0, axis=0, keepdims=True), m8_1))

    def latent_out(hh):
        inv_l = 1.0 / jnp.sum(l_scr[hh], axis=0, keepdims=True)
        return (ot_scr[hh] * inv_l).T

    attn = jnp.concatenate(
        [_dot(jnp.concatenate([latent_out(2 * j), latent_out(2 * j + 1)], axis=1).astype(BF16), wv_ref[j])
         for j in range(N_HEADS // 2)], axis=1)
    y_mla = _dot((attn * gate_mla).astype(BF16), wpb_ref[...])

    h_all = jnp.concatenate([h_main, h_halo], axis=0).astype(BF16)
    has_prev = (i > 0).astype(F32)
    has_next = (i < pl.num_programs(1) - 1).astype(F32)
    n_ext = t + 2 * POOL_HALO
    gw = uext_scr.shape[1] // len(POOL_WINDOWS)
    uext_scr[n_ext:n_ext + SUBLANES] = jnp.zeros((SUBLANES, uext_scr.shape[1]), F32)
    for lv in range(lvl_scr.shape[0]):
        lvl_scr[lv, n_ext:n_ext + SUBLANES] = jnp.zeros((SUBLANES, gw), F32)
    pos = i * t + lax.broadcasted_iota(jnp.int32, (t, LANES), 0)

    def group_inputs(g):
        cols = slice(g * gw, (g + 1) * gw)
        return _dot(h_all, wu_ref[:, cols]), _dot(h_scr[...], wgp_ref[:, cols])

    y_pool = jnp.zeros((t, wpa_ref.shape[1]), F32)
    nxt = group_inputs(0)
    for g, w in enumerate(POOL_WINDOWS):
        hw = w // 2
        cols = slice(g * gw, (g + 1) * gw)
        u_g, gp_g = nxt
        if g + 1 < len(POOL_WINDOWS):
            nxt = group_inputs(g + 1)
        uext_scr[0:POOL_HALO, cols] = u_g[t:t + POOL_HALO] * has_prev
        uext_scr[POOL_HALO:POOL_HALO + t, cols] = u_g[0:t]
        uext_scr[POOL_HALO + t:n_ext, cols] = u_g[t + POOL_HALO:n_ext] * has_next

        def rows_of(k, n, lv=None):
            return uext_scr[pl.ds(k, n), cols] if lv is None else lvl_scr[lv, pl.ds(k, n)]
        lv, span = None, 1
        while span < hw:
            nlv = 0 if lv is None else lv + 1
            lvl_scr[nlv, 0:n_ext] = rows_of(0, n_ext, lv) + rows_of(span, n_ext, lv)
            lv, span = nlv, 2 * span
        acc = rows_of(POOL_HALO - hw, t, lv) + rows_of(POOL_HALO, t, lv)
        cnt = (jnp.minimum(pos + hw, seq_len) - jnp.maximum(pos - hw, 0)).astype(F32)
        inv_cnt = jnp.concatenate([1.0 / cnt] * (gw // LANES), axis=1)
        d = acc * inv_cnt - uext_scr[pl.ds(POOL_HALO, t), cols]
        mixed = _dot(d.astype(BF16), wpool_ref[g]) * ps_ref[:, cols]
        y_pool = y_pool + _dot((mixed * _silu(gp_g)).astype(BF16), wpa_ref[cols, :])

    g_pool = _sigmoid(_dot(h_scr[...], wmp_ref[...]) + bg_ref[0:1])
    g_mla = _sigmoid(_dot(h_scr[...], wmm_ref[...]) + bg_ref[1:2])
    z = (g_pool * y_pool + g_mla * y_mla).astype(BF16)
    out = x + gate * _dot(z, wout_ref[...])
    o_ref[0] = _rms(out) * fg_ref[...]


def _const_spec(shape):
    zeros = (0,) * len(shape)
    return pl.BlockSpec(shape, lambda b, i: zeros, pipeline_mode=pl.Buffered(1))


def _fused_block(x, mod, tab_q, ksh, ckvt, ksh_ctx, ckvt_ctx, vecs, weights):
    bsz, seq, d = x.shape
    t = TOKEN_TILE
    n_lat, n_ctx = ksh.shape[1], ksh_ctx.shape[1]
    keys = n_lat + n_ctx
    halo_blocks = t // POOL_HALO
    last_halo = seq // POOL_HALO - 1
    in_specs = [
        pl.BlockSpec((1, t, d), lambda b, i: (b, i, 0)),
        pl.BlockSpec((1, POOL_HALO, d), lambda b, i: (b, jnp.maximum(i * halo_blocks - 1, 0), 0)),
        pl.BlockSpec((1, POOL_HALO, d), lambda b, i: (b, jnp.minimum((i + 1) * halo_blocks, last_halo), 0)),
        pl.BlockSpec((1, 3, d), lambda b, i: (b, 0, 0)),
        pl.BlockSpec((t, LANES), lambda b, i: (i, 0)),
        pl.BlockSpec((1, n_lat, MXU_DIM), lambda b, i: (b, 0, 0)),
        pl.BlockSpec((1, LANES, n_lat), lambda b, i: (b, 0, 0)),
        pl.BlockSpec((1, n_ctx, MXU_DIM), lambda b, i: (b, 0, 0)),
        pl.BlockSpec((1, LANES, n_ctx), lambda b, i: (b, 0, 0)),
    ]
    in_specs += [_const_spec(v.shape) for v in vecs]
    in_specs += [_const_spec(w.shape) for w in weights]
    exp2_scale = float((QK_NOPE + QK_ROPE) ** -0.5 * math.log2(math.e))
    return pl.pallas_call(
        functools.partial(_block_kernel, seq_len=seq, exp2_scale=exp2_scale),
        grid=(bsz, seq // t),
        in_specs=in_specs,
        out_specs=pl.BlockSpec((1, t, d), lambda b, i: (b, i, 0)),
        out_shape=jax.ShapeDtypeStruct((bsz, seq, d), F32),
        scratch_shapes=[pltpu.VMEM((t, d), BF16),
                        pltpu.VMEM((N_HEADS, t, MXU_DIM), BF16),
                        pltpu.VMEM((N_HEADS, LANES, t), F32),
                        pltpu.VMEM((N_HEADS, SUBLANES, t), F32),
                        pltpu.VMEM((t + 2 * POOL_HALO + SUBLANES, d), F32),
                        pltpu.VMEM((len(POOL_WINDOWS) - 1, t + 2 * POOL_HALO + SUBLANES,
                                    d // len(POOL_WINDOWS)), F32),
                        pltpu.VMEM((keys, t), F32),
                        pltpu.VMEM((keys, t), F32)],
        compiler_params=pltpu.CompilerParams(
            dimension_semantics=("arbitrary", "arbitrary"),
            vmem_limit_bytes=VMEM_LIMIT_BYTES),
        name="fused_block",
    )(x, x, x, mod, tab_q, ksh, ckvt, ksh_ctx, ckvt_ctx, *vecs, *weights)


def _rope_tables(n_tokens, n_plain):
    t = jnp.arange(n_tokens)
    t_row = (t // GRID_W).astype(F32)
    t_col = (t % GRID_W).astype(F32)
    half = QK_ROPE // 2
    inv = 1.0 / (ROPE_THETA ** (jnp.arange(0, half, 2, dtype=F32) / half))
    ang_r = t_row[:, None] * inv
    ang_c = t_col[:, None] * inv
    ang = jnp.concatenate([ang_r, ang_r, ang_c, ang_c], axis=-1)
    cos, sin = jnp.cos(ang), jnp.sin(ang)
    quarter = QK_ROPE // 4
    first = (jnp.arange(QK_ROPE) % (2 * quarter)) < quarter
    sin_a = jnp.where(first, -sin, 0.0)
    sin_b = jnp.where(first, 0.0, sin)
    tabs = jnp.stack([cos, sin_a, sin_b])
    plain = jnp.stack([jnp.ones((n_plain, QK_ROPE), F32),
                       jnp.zeros((n_plain, QK_ROPE), F32),
                       jnp.zeros((n_plain, QK_ROPE), F32)])
    tabs = jnp.concatenate([tabs, plain], axis=1)
    return jnp.pad(tabs, ((0, 0), (0, 0), (0, LANES - QK_ROPE))), jnp.concatenate([cos, sin], axis=-1)


def kernel(x, c, ctx, c_ctx, ada_w, ada_b, norm_g, w_in, b_gate, w_pool, pool_scale, q_norm_g, w_qb,
           kv_norm_g, w_kvb, w_proj_pool, w_proj_mla, w_out, final_g):
    assert ada_w.shape[0] == 1, "single-layer block"
    bsz, seq, d = x.shape
    ctx_len = ctx.shape[1]
    q_lora = q_norm_g.shape[1]
    kv_lora = kv_norm_g.shape[1]
    pool_w = pool_scale.shape[1]
    mla_w = w_proj_mla.shape[1]
    assert kv_lora == LANES and QK_NOPE == LANES and V_DIM == LANES and 2 * QK_ROPE == LANES

    cond = jnp.concatenate([c, c_ctx[None, :]], axis=0)
    mod = _ada_params(cond, ada_w[0], ada_b[0]).reshape(bsz + 1, 3, d)

    splits = (pool_w, pool_w, q_lora, kv_lora + QK_ROPE, mla_w, d, d)
    offs = [0]
    for s in splits:
        offs.append(offs[-1] + s)
    w_in0 = w_in[0]
    wu, wgp, wqa, wkva, wgm, wmp, wmm = [w_in0[:, offs[k]:offs[k + 1]] for k in range(7)]
    wkv_pad = jnp.pad(wkva, ((0, 0), (0, MXU_DIM - wkva.shape[1]))).astype(BF16)

    wqb3 = w_qb[0].reshape(q_lora, N_HEADS, QK_NOPE + QK_ROPE)
    wkvb3 = w_kvb[0].reshape(kv_lora, N_HEADS, QK_NOPE + V_DIM)
    wk_t = jnp.transpose(wkvb3[:, :, :QK_NOPE], (1, 2, 0))
    wv = jnp.transpose(wkvb3[:, :, QK_NOPE:], (1, 0, 2))
    wq_abs = _fold_key_proj(wqb3[:, :, :QK_NOPE].reshape(q_lora, N_HEADS * QK_NOPE), wk_t)
    lane = jnp.arange(QK_ROPE)
    first = (lane % (QK_ROPE // 2)) < QK_ROPE // 4
    partner = jnp.where(first, lane + QK_ROPE // 4, lane - QK_ROPE // 4)
    sign = jnp.where(first, -1.0, 1.0).astype(F32)
    wq_rope = wqb3[:, :, QK_NOPE:]
    wq_cat = jnp.concatenate([wq_abs.reshape(q_lora, N_HEADS, kv_lora),
                              wq_rope, wq_rope[:, :, partner] * sign], axis=2)
    wq_cat = wq_cat.reshape(q_lora, N_HEADS * MXU_DIM).astype(BF16)

    wv2 = wv.reshape(N_HEADS // 2, 2, kv_lora, V_DIM)
    zblk = jnp.zeros_like(wv2[:, 0])
    wv_bd = jnp.concatenate([jnp.concatenate([wv2[:, 0], zblk], axis=-1),
                             jnp.concatenate([zblk, wv2[:, 1]], axis=-1)], axis=1).astype(BF16)

    tabs, tab_q = _rope_tables(seq, ctx_len)
    ksh, ckvt = _shared_kv(x, mod, None, norm_g, wkv_pad, kv_norm_g, tabs[:, :seq], KV_TILE)
    ksh_ctx, ckvt_ctx = _shared_kv(ctx, mod, bsz, norm_g, wkv_pad, kv_norm_g, tabs[:, seq:], ctx_len)

    vecs = (norm_g, q_norm_g, b_gate[0].reshape(2, d), pool_scale, final_g.reshape(1, d))
    weights = (wu.astype(BF16), wgp.astype(BF16), wqa.astype(BF16), wq_cat, wgm.astype(BF16),
               wmp.astype(BF16), wmm.astype(BF16), w_pool[0].astype(BF16), wv_bd,
               w_proj_pool[0].astype(BF16), w_proj_mla[0].astype(BF16), w_out[0].astype(BF16))
    return _fused_block(x, mod, tab_q, ksh, ckvt, ksh_ctx, ckvt_ctx, vecs, weights)
```

```python
import functools
import math

import jax
import jax.numpy as jnp
from jax import lax
from jax.experimental import pallas as pl
from jax.experimental.pallas import tpu as pltpu

GRID_W = 64
N_HEADS = 16
QK_NOPE = 128
QK_ROPE = 64
V_DIM = 128
POOL_WINDOWS = (2, 4, 8, 16)
ROPE_THETA = 10000.0
NORM_EPS = 1e-6

LANES = 128
SUBLANES = 8
MXU_DIM = 256
VMEM_LIMIT_BYTES = 56 * 1024 * 1024

TOKEN_TILE = 256
KV_TILE = 1024
KV_VMEM_LIMIT_BYTES = 40 * 1024 * 1024
KV_ROW_CHUNK = 256
POOL_HALO = max(POOL_WINDOWS) // 2
SCORE_LEAD = 2
HEADS_PER_TRIP = 6

F32 = jnp.float32
BF16 = jnp.bfloat16


def _sigmoid(v):
    return 0.5 + 0.5 * jnp.tanh(0.5 * v)


def _silu(v):
    hv = 0.5 * v
    return hv + hv * jnp.tanh(hv)


def _rms(v):
    return v * lax.rsqrt(jnp.mean(v * v, axis=-1, keepdims=True) + NORM_EPS)


def _dot(a, b):
    return jnp.dot(a, b, preferred_element_type=F32)


def _rope(v, tab_ref):
    return (v * tab_ref[0]
            + pltpu.roll(v, LANES - QK_ROPE // 4, 1) * tab_ref[1]
            + pltpu.roll(v, QK_ROPE // 4, 1) * tab_ref[2])


def _ada_kernel(cond_ref, w_ref, b_ref, o_ref):
    s = _silu(cond_ref[...]).astype(BF16)
    o_ref[...] = _dot(s, w_ref[...].astype(BF16)) + b_ref[...]


def _ada_params(cond, w, b):
    rows, d = cond.shape
    n = w.shape[1]
    bn = d
    return pl.pallas_call(
        _ada_kernel,
        grid=(n // bn,),
        in_specs=[pl.BlockSpec((rows, d), lambda j: (0, 0)),
                  pl.BlockSpec((d, bn), lambda j: (0, j)),
                  pl.BlockSpec((1, bn), lambda j: (0, j))],
        out_specs=pl.BlockSpec((rows, bn), lambda j: (0, j)),
        out_shape=jax.ShapeDtypeStruct((rows, n), F32),
        name="ada_params",
    )(cond, w, b.reshape(1, n))


def _qabs_kernel(wq_ref, wkt_ref, o_ref):
    o_ref[...] = jnp.dot(wq_ref[...], wkt_ref[0], precision=lax.Precision.HIGHEST,
                         preferred_element_type=F32)


def _fold_key_proj(wq_nope, wk_t):
    q_lora = wq_nope.shape[0]
    kv_lora = wk_t.shape[2]
    return pl.pallas_call(
        _qabs_kernel,
        grid=(N_HEADS,),
        in_specs=[pl.BlockSpec((q_lora, QK_NOPE), lambda h: (0, h)),
                  pl.BlockSpec((1, QK_NOPE, kv_lora), lambda h: (h, 0, 0))],
        out_specs=pl.BlockSpec((q_lora, kv_lora), lambda h: (0, h)),
        out_shape=jax.ShapeDtypeStruct((q_lora, N_HEADS * kv_lora), F32),
        name="fold_key_proj",
    )(wq_nope, wk_t)


def _kv_kernel(x_ref, mod_ref, ng_ref, wkv_ref, kvg_ref, tab_ref, ksh_ref, ckvt_ref):
    mod = mod_ref[0]
    gain, shift = ng_ref[...] * (1.0 + mod[1:2]), mod[0:1]
    rows_total = x_ref.shape[1]
    step = min(rows_total, KV_ROW_CHUNK)
    for r0 in range(0, rows_total, step):
        rows = slice(r0, r0 + step)
        h = (_rms(x_ref[0, rows, :]) * gain + shift).astype(BF16)
        kv = _dot(h, wkv_ref[...])
        ckvn = _rms(kv[:, :LANES]) * kvg_ref[...]
        tab = tab_ref.at[:, rows, :]
        kr = _rope(kv[:, LANES:], tab)
        kr = kr + pltpu.roll(kr, QK_ROPE, 1)
        ksh_ref[0, rows, :] = jnp.concatenate([ckvn, kr], axis=1).astype(BF16)
        ckvt_ref[0, :, rows] = ckvn.T.astype(BF16)


def _shared_kv(tokens, mod, mod_row, norm_g, wkv, kv_norm_g, tabs, tile):
    bsz, n, d = tokens.shape
    assert n % tile == 0
    mod_idx = (lambda b, j: (b, 0, 0)) if mod_row is None else (lambda b, j: (mod_row, 0, 0))
    return pl.pallas_call(
        _kv_kernel,
        grid=(bsz, n // tile),
        in_specs=[pl.BlockSpec((1, tile, d), lambda b, j: (b, j, 0)),
                  pl.BlockSpec((1, 3, d), mod_idx),
                  pl.BlockSpec((1, d), lambda b, j: (0, 0)),
                  pl.BlockSpec((d, MXU_DIM), lambda b, j: (0, 0)),
                  pl.BlockSpec((1, LANES), lambda b, j: (0, 0)),
                  pl.BlockSpec((3, tile, LANES), lambda b, j: (0, j, 0))],
        out_specs=[pl.BlockSpec((1, tile, MXU_DIM), lambda b, j: (b, j, 0)),
                   pl.BlockSpec((1, LANES, tile), lambda b, j: (b, 0, j))],
        out_shape=[jax.ShapeDtypeStruct((bsz, n, MXU_DIM), BF16),
                   jax.ShapeDtypeStruct((bsz, LANES, n), BF16)],
        compiler_params=pltpu.CompilerParams(vmem_limit_bytes=KV_VMEM_LIMIT_BYTES),
        name="shared_kv",
    )(tokens, mod, norm_g, wkv, kv_norm_g, tabs)


def _block_kernel(x_ref, xp_ref, xn_ref, mod_ref, tab_ref, ksh_ref, ckvt_ref, kshc_ref, ckvtc_ref,
                  ng_ref, qg_ref, bg_ref, ps_ref, fg_ref,
                  wu_ref, wgp_ref, wqa_ref, wqc_ref, wgm_ref, wmp_ref, wmm_ref,
                  wpool_ref, wv_ref, wpa_ref, wpb_ref, wout_ref,
                  o_ref, h_scr, qc_scr, ot_scr, l_scr, uext_scr, lvl_scr, s0_scr, s1_scr, *,
                  seq_len, exp2_scale):
    t = TOKEN_TILE
    i = pl.program_id(1)
    mod = mod_ref[0]
    shift, gate = mod[0:1], mod[2:3]
    gain = ng_ref[...] * (1.0 + mod[1:2])

    def mod_norm(v):
        return _rms(v) * gain + shift

    x = x_ref[0]
    h_main = mod_norm(x)
    h_halo = mod_norm(jnp.concatenate([xp_ref[0], xn_ref[0]], axis=0))
    hb = h_main.astype(BF16)
    h_scr[...] = hb

    qa = _dot(hb, wqa_ref[...])
    gate_mla = _silu(_dot(hb, wgm_ref[...]))
    q_all = _dot((_rms(qa) * qg_ref[...]).astype(BF16), wqc_ref[...]) * exp2_scale
    for hh in range(N_HEADS):
        c0 = hh * MXU_DIM
        qc_scr[hh] = jnp.concatenate([q_all[:, c0:c0 + LANES],
                                      q_all[:, c0 + LANES:c0 + MXU_DIM] * tab_ref[...]],
                                     axis=1).astype(BF16)

    keys = ksh_ref.shape[1] + kshc_ref.shape[1]
    n_lat_chunks = ksh_ref.shape[1] // MXU_DIM
    kc = MXU_DIM
    n_chunks = keys // kc

    def score_chunk(c, hh, s_scr, m8):
        rows = slice(c * kc, (c + 1) * kc)
        k_c = (ksh_ref[0, rows, :] if c < n_lat_chunks
               else kshc_ref[0, (c - n_lat_chunks) * kc:(c - n_lat_chunks + 1) * kc, :])
        sc = lax.dot_general(k_c, qc_scr[hh], (((1,), (1,)), ((), ())),
                             preferred_element_type=F32)
        s_scr[rows, :] = sc
        return jnp.maximum(m8, jnp.max(sc.reshape(kc // SUBLANES, SUBLANES, t), axis=0))

    def value_chunk(c, s_scr, m, l8, ot):
        rows = slice(c * kc, (c + 1) * kc)
        p = jnp.exp2(s_scr[rows, :] - m)
        l8 = l8 + jnp.sum(p.reshape(kc // SUBLANES, SUBLANES, t), axis=0)
        v_c = (ckvt_ref[0, :, rows] if c < n_lat_chunks
               else ckvtc_ref[0, :, (c - n_lat_chunks) * kc:(c - n_lat_chunks + 1) * kc])
        return l8, ot + _dot(v_c, p.astype(BF16))

    neg_inf8 = jnp.full((SUBLANES, t), -jnp.inf, F32)

    def head_step(hh, h1, h2, s_cur, s_oth, m_cur, m8_1):
        m8_2 = neg_inf8
        l8 = jnp.zeros((SUBLANES, t), F32)
        ot = jnp.zeros((LANES, t), F32)
        for c in range(n_chunks):
            ahead = c + SCORE_LEAD
            if ahead < n_chunks:
                if h1 is not None:
                    m8_1 = score_chunk(ahead, h1, s_oth, m8_1)
            elif h2 is not None:
                m8_2 = score_chunk(ahead - n_chunks, h2, s_cur, m8_2)
            l8, ot = value_chunk(c, s_cur, m_cur, l8, ot)
        ot_scr[hh] = ot
        l_scr[hh] = l8
        return jnp.max(m8_1, axis=0, keepdims=True), m8_2

    def head_trip(first, n_heads, carry):
        m_cur, m8_next = carry
        bufs = (s0_scr, s1_scr)
        for k in range(n_heads):
            hh = first + k
            if isinstance(first, int):
                h1 = hh + 1 if hh + 1 < N_HEADS else None
                h2 = hh + 2 if hh + 2 < N_HEADS else None
            else:
                h1, h2 = hh + 1, hh + 2
            m_cur, m8_next = head_step(hh, h1, h2, bufs[k % 2], bufs[(k + 1) % 2], m_cur, m8_next)
        return m_cur, m8_next

    m8_0, m8_1 = neg_inf8, neg_inf8
    for c in range(n_chunks):
        m8_0 = score_chunk(c, 0, s0_scr, m8_0)
    for c in range(SCORE_LEAD):
        m8_1 = score_chunk(c, 1, s1_scr, m8_1)
    n_trips = (N_HEADS - 1) // HEADS_PER_TRIP
    carry = lax.fori_loop(0, n_trips, lambda j, cr: head_trip(HEADS_PER_TRIP * j, HEADS_PER_TRIP, cr),
                          (jnp.max(m8_0, axis=0, keepdims=True), m8_1))
    head_trip(n_trips * HEADS_PER_TRIP, N_HEADS - n_trips * HEADS_PER_TRIP, carry)

    def latent_out(hh):
        inv_l = 1.0 / jnp.sum(l_scr[hh], axis=0, keepdims=True)
        return (ot_scr[hh] * inv_l).T

    attn = jnp.concatenate(
        [_dot(jnp.concatenate([latent_out(2 * j), latent_out(2 * j + 1)], axis=1).astype(BF16), wv_ref[j])
         for j in range(N_HEADS // 2)], axis=1)
    y_mla = _dot((attn * gate_mla).astype(BF16), wpb_ref[...])

    h_all = jnp.concatenate([h_main, h_halo], axis=0).astype(BF16)
    has_prev = (i > 0).astype(F32)
    has_next = (i < pl.num_programs(1) - 1).astype(F32)
    n_ext = t + 2 * POOL_HALO
    gw = uext_scr.shape[1] // len(POOL_WINDOWS)
    uext_scr[n_ext:n_ext + SUBLANES] = jnp.zeros((SUBLANES, uext_scr.shape[1]), F32)
    for lv in range(lvl_scr.shape[0]):
        lvl_scr[lv, n_ext:n_ext + SUBLANES] = jnp.zeros((SUBLANES, gw), F32)
    pos = i * t + lax.broadcasted_iota(jnp.int32, (t, LANES), 0)

    def group_inputs(g):
        cols = slice(g * gw, (g + 1) * gw)
        return _dot(h_all, wu_ref[:, cols]), _dot(h_scr[...], wgp_ref[:, cols])

    y_pool = jnp.zeros((t, wpa_ref.shape[1]), F32)
    nxt = group_inputs(0)
    for g, w in enumerate(POOL_WINDOWS):
        hw = w // 2
        cols = slice(g * gw, (g + 1) * gw)
        u_g, gp_g = nxt
        if g + 1 < len(POOL_WINDOWS):
            nxt = group_inputs(g + 1)
        uext_scr[0:POOL_HALO, cols] = u_g[t:t + POOL_HALO] * has_prev
        uext_scr[POOL_HALO:POOL_HALO + t, cols] = u_g[0:t]
        uext_scr[POOL_HALO + t:n_ext, cols] = u_g[t + POOL_HALO:n_ext] * has_next

        def rows_of(k, n, lv=None):
            return uext_scr[pl.ds(k, n), cols] if lv is None else lvl_scr[lv, pl.ds(k, n)]
        lv, span = None, 1
        while span < hw:
            nlv = 0 if lv is None else lv + 1
            lvl_scr[nlv, 0:n_ext] = rows_of(0, n_ext, lv) + rows_of(span, n_ext, lv)
            lv, span = nlv, 2 * span
        acc = rows_of(POOL_HALO - hw, t, lv) + rows_of(POOL_HALO, t, lv)
        cnt = (jnp.minimum(pos + hw, seq_len) - jnp.maximum(pos - hw, 0)).astype(F32)
        inv_cnt = jnp.concatenate([1.0 / cnt] * (gw // LANES), axis=1)
        d = acc * inv_cnt - uext_scr[pl.ds(POOL_HALO, t), cols]
        mixed = _dot(d.astype(BF16), wpool_ref[g]) * ps_ref[:, cols]
        y_pool = y_pool + _dot((mixed * _silu(gp_g)).astype(BF16), wpa_ref[cols, :])

    g_pool = _sigmoid(_dot(h_scr[...], wmp_ref[...]) + bg_ref[0:1])
    g_mla = _sigmoid(_dot(h_scr[...], wmm_ref[...]) + bg_ref[1:2])
    z = (g_pool * y_pool + g_mla * y_mla).astype(BF16)
    out = x + gate * _dot(z, wout_ref[...])
    o_ref[0] = _rms(out) * fg_ref[...]


def _const_spec(shape):
    zeros = (0,) * len(shape)
    return pl.BlockSpec(shape, lambda b, i: zeros, pipeline_mode=pl.Buffered(1))


def _fused_block(x, mod, tab_q, ksh, ckvt, ksh_ctx, ckvt_ctx, vecs, weights):
    bsz, seq, d = x.shape
    t = TOKEN_TILE
    n_lat, n_ctx = ksh.shape[1], ksh_ctx.shape[1]
    keys = n_lat + n_ctx
    halo_blocks = t // POOL_HALO
    last_halo = seq // POOL_HALO - 1
    in_specs = [
        pl.BlockSpec((1, t, d), lambda b, i: (b, i, 0)),
        pl.BlockSpec((1, POOL_HALO, d), lambda b, i: (b, jnp.maximum(i * halo_blocks - 1, 0), 0)),
        pl.BlockSpec((1, POOL_HALO, d), lambda b, i: (b, jnp.minimum((i + 1) * halo_blocks, last_halo), 0)),
        pl.BlockSpec((1, 3, d), lambda b, i: (b, 0, 0)),
        pl.BlockSpec((t, LANES), lambda b, i: (i, 0)),
        pl.BlockSpec((1, n_lat, MXU_DIM), lambda b, i: (b, 0, 0)),
        pl.BlockSpec((1, LANES, n_lat), lambda b, i: (b, 0, 0)),
        pl.BlockSpec((1, n_ctx, MXU_DIM), lambda b, i: (b, 0, 0)),
        pl.BlockSpec((1, LANES, n_ctx), lambda b, i: (b, 0, 0)),
    ]
    in_specs += [_const_spec(v.shape) for v in vecs]
    in_specs += [_const_spec(w.shape) for w in weights]
    exp2_scale = float((QK_NOPE + QK_ROPE) ** -0.5 * math.log2(math.e))
    return pl.pallas_call(
        functools.partial(_block_kernel, seq_len=seq, exp2_scale=exp2_scale),
        grid=(bsz, seq // t),
        in_specs=in_specs,
        out_specs=pl.BlockSpec((1, t, d), lambda b, i: (b, i, 0)),
        out_shape=jax.ShapeDtypeStruct((bsz, seq, d), F32),
        scratch_shapes=[pltpu.VMEM((t, d), BF16),
                        pltpu.VMEM((N_HEADS, t, MXU_DIM), BF16),
                        pltpu.VMEM((N_HEADS, LANES, t), F32),
                        pltpu.VMEM((N_HEADS, SUBLANES, t), F32),
                        pltpu.VMEM((t + 2 * POOL_HALO + SUBLANES, d), F32),
                        pltpu.VMEM((len(POOL_WINDOWS) - 1, t + 2 * POOL_HALO + SUBLANES,
                                    d // len(POOL_WINDOWS)), F32),
                        pltpu.VMEM((keys, t), F32),
                        pltpu.VMEM((keys, t), F32)],
        compiler_params=pltpu.CompilerParams(
            dimension_semantics=("arbitrary", "arbitrary"),
            vmem_limit_bytes=VMEM_LIMIT_BYTES),
        name="fused_block",
    )(x, x, x, mod, tab_q, ksh, ckvt, ksh_ctx, ckvt_ctx, *vecs, *weights)


def _rope_tables(n_tokens, n_plain):
    t = jnp.arange(n_tokens)
    t_row = (t // GRID_W).astype(F32)
    t_col = (t % GRID_W).astype(F32)
    half = QK_ROPE // 2
    inv = 1.0 / (ROPE_THETA ** (jnp.arange(0, half, 2, dtype=F32) / half))
    ang_r = t_row[:, None] * inv
    ang_c = t_col[:, None] * inv
    ang = jnp.concatenate([ang_r, ang_r, ang_c, ang_c], axis=-1)
    cos, sin = jnp.cos(ang), jnp.sin(ang)
    quarter = QK_ROPE // 4
    first = (jnp.arange(QK_ROPE) % (2 * quarter)) < quarter
    sin_a = jnp.where(first, -sin, 0.0)
    sin_b = jnp.where(first, 0.0, sin)
    tabs = jnp.stack([cos, sin_a, sin_b])
    plain = jnp.stack([jnp.ones((n_plain, QK_ROPE), F32),
                       jnp.zeros((n_plain, QK_ROPE), F32),
                       jnp.zeros((n_plain, QK_ROPE), F32)])
    tabs = jnp.concatenate([tabs, plain], axis=1)
    return jnp.pad(tabs, ((0, 0), (0, 0), (0, LANES - QK_ROPE))), jnp.concatenate([cos, sin], axis=-1)


def kernel(x, c, ctx, c_ctx, ada_w, ada_b, norm_g, w_in, b_gate, w_pool, pool_scale, q_norm_g, w_qb,
           kv_norm_g, w_kvb, w_proj_pool, w_proj_mla, w_out, final_g):
    assert ada_w.shape[0] == 1, "single-layer block"
    bsz, seq, d = x.shape
    ctx_len = ctx.shape[1]
    q_lora = q_norm_g.shape[1]
    kv_lora = kv_norm_g.shape[1]
    pool_w = pool_scale.shape[1]
    mla_w = w_proj_mla.shape[1]
    assert kv_lora == LANES and QK_NOPE == LANES and V_DIM == LANES and 2 * QK_ROPE == LANES

    cond = jnp.concatenate([c, c_ctx[None, :]], axis=0)
    mod = _ada_params(cond, ada_w[0], ada_b[0]).reshape(bsz + 1, 3, d)

    splits = (pool_w, pool_w, q_lora, kv_lora + QK_ROPE, mla_w, d, d)
    offs = [0]
    for s in splits:
        offs.append(offs[-1] + s)
    w_in0 = w_in[0]
    wu, wgp, wqa, wkva, wgm, wmp, wmm = [w_in0[:, offs[k]:offs[k + 1]] for k in range(7)]
    wkv_pad = jnp.pad(wkva, ((0, 0), (0, MXU_DIM - wkva.shape[1]))).astype(BF16)

    wqb3 = w_qb[0].reshape(q_lora, N_HEADS, QK_NOPE + QK_ROPE)
    wkvb3 = w_kvb[0].reshape(kv_lora, N_HEADS, QK_NOPE + V_DIM)
    wk_t = jnp.transpose(wkvb3[:, :, :QK_NOPE], (1, 2, 0))
    wv = jnp.transpose(wkvb3[:, :, QK_NOPE:], (1, 0, 2))
    wq_abs = _fold_key_proj(wqb3[:, :, :QK_NOPE].reshape(q_lora, N_HEADS * QK_NOPE), wk_t)
    lane = jnp.arange(QK_ROPE)
    first = (lane % (QK_ROPE // 2)) < QK_ROPE // 4
    partner = jnp.where(first, lane + QK_ROPE // 4, lane - QK_ROPE // 4)
    sign = jnp.where(first, -1.0, 1.0).astype(F32)
    wq_rope = wqb3[:, :, QK_NOPE:]
    wq_cat = jnp.concatenate([wq_abs.reshape(q_lora, N_HEADS, kv_lora),
                              wq_rope, wq_rope[:, :, partner] * sign], axis=2)
    wq_cat = wq_cat.reshape(q_lora, N_HEADS * MXU_DIM).astype(BF16)

    wv2 = wv.reshape(N_HEADS // 2, 2, kv_lora, V_DIM)
    zblk = jnp.zeros_like(wv2[:, 0])
    wv_bd = jnp.concatenate([jnp.concatenate([wv2[:, 0], zblk], axis=-1),
                             jnp.concatenate([zblk, wv2[:, 1]], axis=-1)], axis=1).astype(BF16)

    tabs, tab_q = _rope_tables(seq, ctx_len)
    ksh, ckvt = _shared_kv(x, mod, None, norm_g, wkv_pad, kv_norm_g, tabs[:, :seq], KV_TILE)
    ksh_ctx, ckvt_ctx = _shared_kv(ctx, mod, bsz, norm_g, wkv_pad, kv_norm_g, tabs[:, seq:], ctx_len)

    vecs = (norm_g, q_norm_g, b_gate[0].reshape(2, d), pool_scale, final_g.reshape(1, d))
    weights = (wu.astype(BF16), wgp.astype(BF16), wqa.astype(BF16), wq_cat, wgm.astype(BF16),
               wmp.astype(BF16), wmm.astype(BF16), w_pool[0].astype(BF16), wv_bd,
               w_proj_pool[0].astype(BF16), w_proj_mla[0].astype(BF16), w_out[0].astype(BF16))
    return _fused_block(x, mod, tab_q, ksh, ckvt, ksh_ctx, ckvt_ctx, vecs, weights)
```

```python
import functools
import math

import jax
import jax.numpy as jnp
import numpy as np
from jax import lax
from jax.experimental import pallas as pl
from jax.experimental.pallas import tpu as pltpu

GRID_W = 64
N_HEADS = 16
QK_NOPE = 128
QK_ROPE = 64
V_DIM = 128
POOL_WINDOWS = (2, 4, 8, 16)
ROPE_THETA = 10000.0
NORM_EPS = 1e-6

LANES = 128
SUBLANES = 8
MXU_DIM = 256
VMEM_LIMIT_BYTES = 56 * 1024 * 1024

TOKEN_TILE = 256
KV_TILE = 1024
KV_VMEM_LIMIT_BYTES = 40 * 1024 * 1024
KV_ROW_CHUNK = 256
FOLD_HEADS_PER_STEP = 4
POOL_HALO = max(POOL_WINDOWS) // 2
SCORE_LEAD = 2
HEADS_PER_TRIP = 6

F32 = jnp.float32
BF16 = jnp.bfloat16


def _sigmoid(v):
    return 0.5 + 0.5 * jnp.tanh(0.5 * v)


def _silu(v):
    hv = 0.5 * v
    return hv + hv * jnp.tanh(hv)


def _rms(v):
    return v * lax.rsqrt(jnp.mean(v * v, axis=-1, keepdims=True) + NORM_EPS)


def _dot(a, b):
    return jnp.dot(a, b, preferred_element_type=F32)


def _rope(v, tab_ref):
    return (v * tab_ref[0]
            + pltpu.roll(v, LANES - QK_ROPE // 4, 1) * tab_ref[1]
            + pltpu.roll(v, QK_ROPE // 4, 1) * tab_ref[2])


def _ada_kernel(cond_ref, w_ref, b_ref, o_ref):
    s = _silu(cond_ref[...]).astype(BF16)
    o_ref[...] = _dot(s, w_ref[...].astype(BF16)) + b_ref[...]


def _ada_params(cond, w, b):
    rows, d = cond.shape
    n = w.shape[1]
    bn = d
    return pl.pallas_call(
        _ada_kernel,
        grid=(n // bn,),
        in_specs=[pl.BlockSpec((rows, d), lambda j: (0, 0)),
                  pl.BlockSpec((d, bn), lambda j: (0, j)),
                  pl.BlockSpec((1, bn), lambda j: (0, j))],
        out_specs=pl.BlockSpec((rows, bn), lambda j: (0, j)),
        out_shape=jax.ShapeDtypeStruct((rows, n), F32),
        name="ada_params",
    )(cond, w, b.reshape(1, n))


def _qabs_kernel(wq_ref, wkt_ref, o_ref):
    for k in range(FOLD_HEADS_PER_STEP):
        o_ref[:, k * LANES:(k + 1) * LANES] = jnp.dot(
            wq_ref[:, k * QK_NOPE:(k + 1) * QK_NOPE], wkt_ref[k],
            precision=lax.Precision.HIGHEST, preferred_element_type=F32)


def _fold_key_proj(wq_nope, wk_t):
    q_lora = wq_nope.shape[0]
    kv_lora = wk_t.shape[2]
    hs = FOLD_HEADS_PER_STEP
    return pl.pallas_call(
        _qabs_kernel,
        grid=(N_HEADS // hs,),
        in_specs=[pl.BlockSpec((q_lora, hs * QK_NOPE), lambda g: (0, g)),
                  pl.BlockSpec((hs, QK_NOPE, kv_lora), lambda g: (g, 0, 0))],
        out_specs=pl.BlockSpec((q_lora, hs * kv_lora), lambda g: (0, g)),
        out_shape=jax.ShapeDtypeStruct((q_lora, N_HEADS * kv_lora), F32),
        name="fold_key_proj",
    )(wq_nope, wk_t)


def _kv_kernel(x_ref, mod_ref, ng_ref, wkv_ref, kvg_ref, tab_ref, ksh_ref, ckvt_ref):
    mod = mod_ref[0]
    gain, shift = ng_ref[...] * (1.0 + mod[1:2]), mod[0:1]
    rows_total = x_ref.shape[1]
    step = min(rows_total, KV_ROW_CHUNK)
    for r0 in range(0, rows_total, step):
        rows = slice(r0, r0 + step)
        h = (_rms(x_ref[0, rows, :]) * gain + shift).astype(BF16)
        kv = _dot(h, wkv_ref[...])
        ckvn = _rms(kv[:, :LANES]) * kvg_ref[...]
        tab = tab_ref.at[:, rows, :]
        kr = _rope(kv[:, LANES:], tab)
        kr = kr + pltpu.roll(kr, QK_ROPE, 1)
        ksh_ref[0, rows, :] = jnp.concatenate([ckvn, kr], axis=1).astype(BF16)
        ckvt_ref[0, :, rows] = ckvn.T.astype(BF16)


def _shared_kv(tokens, mod, mod_row, norm_g, wkv, kv_norm_g, tabs, tile):
    bsz, n, d = tokens.shape
    assert n % tile == 0
    mod_idx = (lambda b, j: (b, 0, 0)) if mod_row is None else (lambda b, j: (mod_row, 0, 0))
    return pl.pallas_call(
        _kv_kernel,
        grid=(bsz, n // tile),
        in_specs=[pl.BlockSpec((1, tile, d), lambda b, j: (b, j, 0)),
                  pl.BlockSpec((1, 3, d), mod_idx),
                  pl.BlockSpec((1, d), lambda b, j: (0, 0)),
                  pl.BlockSpec((d, MXU_DIM), lambda b, j: (0, 0)),
                  pl.BlockSpec((1, LANES), lambda b, j: (0, 0)),
                  pl.BlockSpec((3, tile, LANES), lambda b, j: (0, j, 0))],
        out_specs=[pl.BlockSpec((1, tile, MXU_DIM), lambda b, j: (b, j, 0)),
                   pl.BlockSpec((1, LANES, tile), lambda b, j: (b, 0, j))],
        out_shape=[jax.ShapeDtypeStruct((bsz, n, MXU_DIM), BF16),
                   jax.ShapeDtypeStruct((bsz, LANES, n), BF16)],
        compiler_params=pltpu.CompilerParams(vmem_limit_bytes=KV_VMEM_LIMIT_BYTES),
        name="shared_kv",
    )(tokens, mod, norm_g, wkv, kv_norm_g, tabs)


def _block_kernel(x_ref, xp_ref, xn_ref, mod_ref, tab_ref, ksh_ref, ckvt_ref, kshc_ref, ckvtc_ref,
                  ng_ref, qg_ref, bg_ref, ps_ref, fg_ref,
                  wu_ref, wgp_ref, wqa_ref, wqc_ref, wgm_ref, wmp_ref, wmm_ref,
                  wpool_ref, wv_ref, wpa_ref, wpb_ref, wout_ref,
                  o_ref, h_scr, qc_scr, ot_scr, l_scr, uext_scr, lvl_scr, s0_scr, s1_scr, *,
                  seq_len, exp2_scale):
    t = TOKEN_TILE
    i = pl.program_id(1)
    mod = mod_ref[0]
    shift, gate = mod[0:1], mod[2:3]
    gain = ng_ref[...] * (1.0 + mod[1:2])

    def mod_norm(v):
        return _rms(v) * gain + shift

    x = x_ref[0]
    h_main = mod_norm(x)
    h_halo = mod_norm(jnp.concatenate([xp_ref[0], xn_ref[0]], axis=0))
    hb = h_main.astype(BF16)
    h_scr[...] = hb

    qa = _dot(hb, wqa_ref[...])
    gate_mla = _silu(_dot(hb, wgm_ref[...]))
    q_all = _dot((_rms(qa) * qg_ref[...]).astype(BF16), wqc_ref[...]) * exp2_scale
    for hh in range(N_HEADS):
        c0 = hh * MXU_DIM
        qc_scr[hh] = jnp.concatenate([q_all[:, c0:c0 + LANES],
                                      q_all[:, c0 + LANES:c0 + MXU_DIM] * tab_ref[...]],
                                     axis=1).astype(BF16)

    keys = ksh_ref.shape[1] + kshc_ref.shape[1]
    n_lat_chunks = ksh_ref.shape[1] // MXU_DIM
    kc = MXU_DIM
    n_chunks = keys // kc

    def score_chunk(c, hh, s_scr, m8):
        rows = slice(c * kc, (c + 1) * kc)
        k_c = (ksh_ref[0, rows, :] if c < n_lat_chunks
               else kshc_ref[0, (c - n_lat_chunks) * kc:(c - n_lat_chunks + 1) * kc, :])
        sc = lax.dot_general(k_c, qc_scr[hh], (((1,), (1,)), ((), ())),
                             preferred_element_type=F32)
        s_scr[rows, :] = sc
        return jnp.maximum(m8, jnp.max(sc.reshape(kc // SUBLANES, SUBLANES, t), axis=0))

    def value_chunk(c, s_scr, m, l8, ot):
        rows = slice(c * kc, (c + 1) * kc)
        p = jnp.exp2(s_scr[rows, :] - m)
        l8 = l8 + jnp.sum(p.reshape(kc // SUBLANES, SUBLANES, t), axis=0)
        v_c = (ckvt_ref[0, :, rows] if c < n_lat_chunks
               else ckvtc_ref[0, :, (c - n_lat_chunks) * kc:(c - n_lat_chunks + 1) * kc])
        return l8, ot + _dot(v_c, p.astype(BF16))

    neg_inf8 = jnp.full((SUBLANES, t), -jnp.inf, F32)

    def head_step(hh, h1, h2, s_cur, s_oth, m_cur, m8_1):
        m8_2 = neg_inf8
        l8 = jnp.zeros((SUBLANES, t), F32)
        ot = jnp.zeros((LANES, t), F32)
        for c in range(n_chunks):
            ahead = c + SCORE_LEAD
            if ahead < n_chunks:
                if h1 is not None:
                    m8_1 = score_chunk(ahead, h1, s_oth, m8_1)
            elif h2 is not None:
                m8_2 = score_chunk(ahead - n_chunks, h2, s_cur, m8_2)
            l8, ot = value_chunk(c, s_cur, m_cur, l8, ot)
        ot_scr[hh] = ot
        l_scr[hh] = l8
        return jnp.max(m8_1, axis=0, keepdims=True), m8_2

    def head_trip(first, n_heads, carry):
        m_cur, m8_next = carry
        bufs = (s0_scr, s1_scr)
        for k in range(n_heads):
            hh = first + k
            if isinstance(first, int):
                h1 = hh + 1 if hh + 1 < N_HEADS else None
                h2 = hh + 2 if hh + 2 < N_HEADS else None
            else:
                h1, h2 = hh + 1, hh + 2
            m_cur, m8_next = head_step(hh, h1, h2, bufs[k % 2], bufs[(k + 1) % 2], m_cur, m8_next)
        return m_cur, m8_next

    m8_0, m8_1 = neg_inf8, neg_inf8
    for c in range(n_chunks):
        m8_0 = score_chunk(c, 0, s0_scr, m8_0)
    for c in range(SCORE_LEAD):
        m8_1 = score_chunk(c, 1, s1_scr, m8_1)
    n_trips = (N_HEADS - 1) // HEADS_PER_TRIP
    carry = lax.fori_loop(0, n_trips, lambda j, cr: head_trip(HEADS_PER_TRIP * j, HEADS_PER_TRIP, cr),
                          (jnp.max(m8_0, axis=0, keepdims=True), m8_1))
    head_trip(n_trips * HEADS_PER_TRIP, N_HEADS - n_trips * HEADS_PER_TRIP, carry)

    def latent_out(hh):
        inv_l = 1.0 / jnp.sum(l_scr[hh], axis=0, keepdims=True)
        return (ot_scr[hh] * inv_l).T

    attn = jnp.concatenate(
        [_dot(jnp.concatenate([latent_out(2 * j), latent_out(2 * j + 1)], axis=1).astype(BF16), wv_ref[j])
         for j in range(N_HEADS // 2)], axis=1)
    y_mla = _dot((attn * gate_mla).astype(BF16), wpb_ref[...])

    h_all = jnp.concatenate([h_main, h_halo], axis=0).astype(BF16)
    has_prev = (i > 0).astype(F32)
    has_next = (i < pl.num_programs(1) - 1).astype(F32)
    n_ext = t + 2 * POOL_HALO
    gw = uext_scr.shape[1] // len(POOL_WINDOWS)
    uext_scr[n_ext:n_ext + SUBLANES] = jnp.zeros((SUBLANES, uext_scr.shape[1]), F32)
    for lv in range(lvl_scr.shape[0]):
        lvl_scr[lv, n_ext:n_ext + SUBLANES] = jnp.zeros((SUBLANES, gw), F32)
    pos = i * t + lax.broadcasted_iota(jnp.int32, (t, LANES), 0)

    def group_inputs(g):
        cols = slice(g * gw, (g + 1) * gw)
        return _dot(h_all, wu_ref[:, cols]), _dot(h_scr[...], wgp_ref[:, cols])

    y_pool = jnp.zeros((t, wpa_ref.shape[1]), F32)
    nxt = group_inputs(0)
    for g, w in enumerate(POOL_WINDOWS):
        hw = w // 2
        cols = slice(g * gw, (g + 1) * gw)
        u_g, gp_g = nxt
        if g + 1 < len(POOL_WINDOWS):
            nxt = group_inputs(g + 1)
        uext_scr[0:POOL_HALO, cols] = u_g[t:t + POOL_HALO] * has_prev
        uext_scr[POOL_HALO:POOL_HALO + t, cols] = u_g[0:t]
        uext_scr[POOL_HALO + t:n_ext, cols] = u_g[t + POOL_HALO:n_ext] * has_next

        def rows_of(k, n, lv=None):
            return uext_scr[pl.ds(k, n), cols] if lv is None else lvl_scr[lv, pl.ds(k, n)]
        lv, span = None, 1
        while span < hw:
            nlv = 0 if lv is None else lv + 1
            lvl_scr[nlv, 0:n_ext] = rows_of(0, n_ext, lv) + rows_of(span, n_ext, lv)
            lv, span = nlv, 2 * span
        acc = rows_of(POOL_HALO - hw, t, lv) + rows_of(POOL_HALO, t, lv)
        cnt = (jnp.minimum(pos + hw, seq_len) - jnp.maximum(pos - hw, 0)).astype(F32)
        inv_cnt = jnp.concatenate([1.0 / cnt] * (gw // LANES), axis=1)
        d = acc * inv_cnt - uext_scr[pl.ds(POOL_HALO, t), cols]
        mixed = _dot(d.astype(BF16), wpool_ref[g]) * ps_ref[:, cols]
        y_pool = y_pool + _dot((mixed * _silu(gp_g)).astype(BF16), wpa_ref[cols, :])

    g_pool = _sigmoid(_dot(h_scr[...], wmp_ref[...]) + bg_ref[0:1])
    g_mla = _sigmoid(_dot(h_scr[...], wmm_ref[...]) + bg_ref[1:2])
    z = (g_pool * y_pool + g_mla * y_mla).astype(BF16)
    out = x + gate * _dot(z, wout_ref[...])
    o_ref[0] = _rms(out) * fg_ref[...]


def _const_spec(shape):
    zeros = (0,) * len(shape)
    return pl.BlockSpec(shape, lambda b, i: zeros, pipeline_mode=pl.Buffered(1))


def _fused_block(x, mod, tab_q, ksh, ckvt, ksh_ctx, ckvt_ctx, vecs, weights):
    bsz, seq, d = x.shape
    t = TOKEN_TILE
    n_lat, n_ctx = ksh.shape[1], ksh_ctx.shape[1]
    keys = n_lat + n_ctx
    halo_blocks = t // POOL_HALO
    last_halo = seq // POOL_HALO - 1
    in_specs = [
        pl.BlockSpec((1, t, d), lambda b, i: (b, i, 0)),
        pl.BlockSpec((1, POOL_HALO, d), lambda b, i: (b, jnp.maximum(i * halo_blocks - 1, 0), 0)),
        pl.BlockSpec((1, POOL_HALO, d), lambda b, i: (b, jnp.minimum((i + 1) * halo_blocks, last_halo), 0)),
        pl.BlockSpec((1, 3, d), lambda b, i: (b, 0, 0)),
        pl.BlockSpec((t, LANES), lambda b, i: (i, 0)),
        pl.BlockSpec((1, n_lat, MXU_DIM), lambda b, i: (b, 0, 0)),
        pl.BlockSpec((1, LANES, n_lat), lambda b, i: (b, 0, 0)),
        pl.BlockSpec((1, n_ctx, MXU_DIM), lambda b, i: (b, 0, 0)),
        pl.BlockSpec((1, LANES, n_ctx), lambda b, i: (b, 0, 0)),
    ]
    in_specs += [_const_spec(v.shape) for v in vecs]
    in_specs += [_const_spec(w.shape) for w in weights]
    exp2_scale = float((QK_NOPE + QK_ROPE) ** -0.5 * math.log2(math.e))
    return pl.pallas_call(
        functools.partial(_block_kernel, seq_len=seq, exp2_scale=exp2_scale),
        grid=(bsz, seq // t),
        in_specs=in_specs,
        out_specs=pl.BlockSpec((1, t, d), lambda b, i: (b, i, 0)),
        out_shape=jax.ShapeDtypeStruct((bsz, seq, d), F32),
        scratch_shapes=[pltpu.VMEM((t, d), BF16),
                        pltpu.VMEM((N_HEADS, t, MXU_DIM), BF16),
                        pltpu.VMEM((N_HEADS, LANES, t), F32),
                        pltpu.VMEM((N_HEADS, SUBLANES, t), F32),
                        pltpu.VMEM((t + 2 * POOL_HALO + SUBLANES, d), F32),
                        pltpu.VMEM((len(POOL_WINDOWS) - 1, t + 2 * POOL_HALO + SUBLANES,
                                    d // len(POOL_WINDOWS)), F32),
                        pltpu.VMEM((keys, t), F32),
                        pltpu.VMEM((keys, t), F32)],
        compiler_params=pltpu.CompilerParams(
            dimension_semantics=("arbitrary", "arbitrary"),
            vmem_limit_bytes=VMEM_LIMIT_BYTES),
        name="fused_block",
    )(x, x, x, mod, tab_q, ksh, ckvt, ksh_ctx, ckvt_ctx, *vecs, *weights)


def _rope_tables(n_tokens, n_plain):
    f32 = np.float32
    t = np.arange(n_tokens)
    t_row = (t // GRID_W).astype(f32)
    t_col = (t % GRID_W).astype(f32)
    half = QK_ROPE // 2
    inv = (f32(1.0) / (f32(ROPE_THETA) ** (np.arange(0, half, 2, dtype=f32) / f32(half)))).astype(f32)
    ang_r = t_row[:, None] * inv
    ang_c = t_col[:, None] * inv
    ang = np.concatenate([ang_r, ang_r, ang_c, ang_c], axis=-1)
    cos, sin = np.cos(ang).astype(f32), np.sin(ang).astype(f32)
    quarter = QK_ROPE // 4
    first = (np.arange(QK_ROPE) % (2 * quarter)) < quarter
    sin_a = np.where(first, -sin, f32(0.0))
    sin_b = np.where(first, f32(0.0), sin)
    tabs = np.stack([cos, sin_a, sin_b])
    plain = np.stack([np.ones((n_plain, QK_ROPE), f32),
                      np.zeros((n_plain, QK_ROPE), f32),
                      np.zeros((n_plain, QK_ROPE), f32)])
    tabs = np.concatenate([tabs, plain], axis=1)
    return np.pad(tabs, ((0, 0), (0, 0), (0, LANES - QK_ROPE))), np.concatenate([cos, sin], axis=-1)


def kernel(x, c, ctx, c_ctx, ada_w, ada_b, norm_g, w_in, b_gate, w_pool, pool_scale, q_norm_g, w_qb,
           kv_norm_g, w_kvb, w_proj_pool, w_proj_mla, w_out, final_g):
    assert ada_w.shape[0] == 1, "single-layer block"
    bsz, seq, d = x.shape
    ctx_len = ctx.shape[1]
    q_lora = q_norm_g.shape[1]
    kv_lora = kv_norm_g.shape[1]
    pool_w = pool_scale.shape[1]
    mla_w = w_proj_mla.shape[1]
    assert kv_lora == LANES and QK_NOPE == LANES and V_DIM == LANES and 2 * QK_ROPE == LANES

    cond = jnp.concatenate([c, c_ctx[None, :]], axis=0)
    mod = _ada_params(cond, ada_w[0], ada_b[0]).reshape(bsz + 1, 3, d)

    splits = (pool_w, pool_w, q_lora, kv_lora + QK_ROPE, mla_w, d, d)
    offs = [0]
    for s in splits:
        offs.append(offs[-1] + s)
    w_in0 = w_in[0]
    wu, wgp, wqa, wkva, wgm, wmp, wmm = [w_in0[:, offs[k]:offs[k + 1]] for k in range(7)]
    wkv_pad = jnp.pad(wkva, ((0, 0), (0, MXU_DIM - wkva.shape[1]))).astype(BF16)

    wqb3 = w_qb[0].reshape(q_lora, N_HEADS, QK_NOPE + QK_ROPE)
    wkvb3 = w_kvb[0].reshape(kv_lora, N_HEADS, QK_NOPE + V_DIM)
    wk_t = jnp.transpose(wkvb3[:, :, :QK_NOPE], (1, 2, 0))
    wv = jnp.transpose(wkvb3[:, :, QK_NOPE:], (1, 0, 2))
    wq_abs = _fold_key_proj(wqb3[:, :, :QK_NOPE].reshape(q_lora, N_HEADS * QK_NOPE), wk_t)
    lane = jnp.arange(QK_ROPE)
    first = (lane % (QK_ROPE // 2)) < QK_ROPE // 4
    partner = jnp.where(first, lane + QK_ROPE // 4, lane - QK_ROPE // 4)
    sign = jnp.where(first, -1.0, 1.0).astype(F32)
    wq_rope = wqb3[:, :, QK_NOPE:]
    wq_cat = jnp.concatenate([wq_abs.reshape(q_lora, N_HEADS, kv_lora),
                              wq_rope, wq_rope[:, :, partner] * sign], axis=2)
    wq_cat = wq_cat.reshape(q_lora, N_HEADS * MXU_DIM).astype(BF16)

    wv2 = wv.reshape(N_HEADS // 2, 2, kv_lora, V_DIM)
    zblk = jnp.zeros_like(wv2[:, 0])
    wv_bd = jnp.concatenate([jnp.concatenate([wv2[:, 0], zblk], axis=-1),
                             jnp.concatenate([zblk, wv2[:, 1]], axis=-1)], axis=1).astype(BF16)

    tabs, tab_q = _rope_tables(seq, ctx_len)
    ksh, ckvt = _shared_kv(x, mod, None, norm_g, wkv_pad, kv_norm_g, tabs[:, :seq], KV_TILE)
    ksh_ctx, ckvt_ctx = _shared_kv(ctx, mod, bsz, norm_g, wkv_pad, kv_norm_g, tabs[:, seq:], ctx_len)

    vecs = (norm_g, q_norm_g, b_gate[0].reshape(2, d), pool_scale, final_g.reshape(1, d))
    weights = (wu.astype(BF16), wgp.astype(BF16), wqa.astype(BF16), wq_cat, wgm.astype(BF16),
               wmp.astype(BF16), wmm.astype(BF16), w_pool[0].astype(BF16), wv_bd,
               w_proj_pool[0].astype(BF16), w_proj_mla[0].astype(BF16), w_out[0].astype(BF16))
    return _fused_block(x, mod, tab_q, ksh, ckvt, ksh_ctx, ckvt_ctx, vecs, weights)
```

```python
import functools
import math

import jax
import jax.numpy as jnp
import numpy as np
from jax import lax
from jax.experimental import pallas as pl
from jax.experimental.pallas import tpu as pltpu

GRID_W = 64
N_HEADS = 16
QK_NOPE = 128
QK_ROPE = 64
V_DIM = 128
POOL_WINDOWS = (2, 4, 8, 16)
ROPE_THETA = 10000.0
NORM_EPS = 1e-6

LANES = 128
SUBLANES = 8
MXU_DIM = 256
VMEM_LIMIT_BYTES = 56 * 1024 * 1024

TOKEN_TILE = 256
KV_TILE = 1024
KV_VMEM_LIMIT_BYTES = 40 * 1024 * 1024
KV_ROW_CHUNK = 256
FOLD_HEADS_PER_STEP = 4
POOL_HALO = max(POOL_WINDOWS) // 2
SCORE_LEAD = 2
HEADS_PER_TRIP = 6

F32 = jnp.float32
BF16 = jnp.bfloat16


def _sigmoid(v):
    return 0.5 + 0.5 * jnp.tanh(0.5 * v)


def _silu(v):
    hv = 0.5 * v
    return hv + hv * jnp.tanh(hv)


def _rms(v):
    return v * lax.rsqrt(jnp.mean(v * v, axis=-1, keepdims=True) + NORM_EPS)


def _dot(a, b):
    return jnp.dot(a, b, preferred_element_type=F32)


def _rope(v, tab_ref):
    return (v * tab_ref[0]
            + pltpu.roll(v, LANES - QK_ROPE // 4, 1) * tab_ref[1]
            + pltpu.roll(v, QK_ROPE // 4, 1) * tab_ref[2])


def _ada_kernel(cond_ref, w_ref, b_ref, o_ref):
    s = _silu(cond_ref[...]).astype(BF16)
    o_ref[...] = _dot(s, w_ref[...].astype(BF16)) + b_ref[...]


def _ada_params(cond, w, b):
    rows, d = cond.shape
    n = w.shape[1]
    bn = d
    return pl.pallas_call(
        _ada_kernel,
        grid=(n // bn,),
        in_specs=[pl.BlockSpec((rows, d), lambda j: (0, 0)),
                  pl.BlockSpec((d, bn), lambda j: (0, j)),
                  pl.BlockSpec((1, bn), lambda j: (0, j))],
        out_specs=pl.BlockSpec((rows, bn), lambda j: (0, j)),
        out_shape=jax.ShapeDtypeStruct((rows, n), F32),
        name="ada_params",
    )(cond, w, b.reshape(1, n))


def _qabs_kernel(wq_ref, wkt_ref, o_ref):
    for k in range(FOLD_HEADS_PER_STEP):
        o_ref[:, k * LANES:(k + 1) * LANES] = jnp.dot(
            wq_ref[:, k * QK_NOPE:(k + 1) * QK_NOPE], wkt_ref[k],
            precision=lax.Precision.HIGHEST, preferred_element_type=F32)


def _fold_key_proj(wq_nope, wk_t):
    q_lora = wq_nope.shape[0]
    kv_lora = wk_t.shape[2]
    hs = FOLD_HEADS_PER_STEP
    return pl.pallas_call(
        _qabs_kernel,
        grid=(N_HEADS // hs,),
        in_specs=[pl.BlockSpec((q_lora, hs * QK_NOPE), lambda g: (0, g)),
                  pl.BlockSpec((hs, QK_NOPE, kv_lora), lambda g: (g, 0, 0))],
        out_specs=pl.BlockSpec((q_lora, hs * kv_lora), lambda g: (0, g)),
        out_shape=jax.ShapeDtypeStruct((q_lora, N_HEADS * kv_lora), F32),
        name="fold_key_proj",
    )(wq_nope, wk_t)


def _kv_kernel(x_ref, mod_ref, ng_ref, wkv_ref, kvg_ref, tab_ref, ksh_ref, ckvt_ref):
    mod = mod_ref[0]
    gain, shift = ng_ref[...] * (1.0 + mod[1:2]), mod[0:1]
    rows_total = x_ref.shape[1]
    step = min(rows_total, KV_ROW_CHUNK)
    chunks = [slice(r0, r0 + step) for r0 in range(0, rows_total, step)]

    def normed(rows):
        return (_rms(x_ref[0, rows, :]) * gain + shift).astype(BF16)

    h = normed(chunks[0])
    for k, rows in enumerate(chunks):
        h_next = normed(chunks[k + 1]) if k + 1 < len(chunks) else None
        kv = _dot(h, wkv_ref[...])
        ckvn = _rms(kv[:, :LANES]) * kvg_ref[...]
        kr = _rope(kv[:, LANES:], tab_ref.at[:, rows, :])
        kr = kr + pltpu.roll(kr, QK_ROPE, 1)
        ksh_ref[0, rows, :] = jnp.concatenate([ckvn, kr], axis=1).astype(BF16)
        ckvt_ref[0, :, rows] = ckvn.T.astype(BF16)
        h = h_next


def _shared_kv(tokens, mod, mod_row, norm_g, wkv, kv_norm_g, tabs, tile):
    bsz, n, d = tokens.shape
    assert n % tile == 0 and tabs.shape[1] in (n, tile)
    mod_idx = (lambda b, j: (b, 0, 0)) if mod_row is None else (lambda b, j: (mod_row, 0, 0))
    tab_idx = (lambda b, j: (0, j, 0)) if tabs.shape[1] == n else (lambda b, j: (0, 0, 0))
    return pl.pallas_call(
        _kv_kernel,
        grid=(bsz, n // tile),
        in_specs=[pl.BlockSpec((1, tile, d), lambda b, j: (b, j, 0)),
                  pl.BlockSpec((1, 3, d), mod_idx),
                  pl.BlockSpec((1, d), lambda b, j: (0, 0)),
                  pl.BlockSpec((d, MXU_DIM), lambda b, j: (0, 0)),
                  pl.BlockSpec((1, LANES), lambda b, j: (0, 0)),
                  pl.BlockSpec((3, tile, LANES), tab_idx)],
        out_specs=[pl.BlockSpec((1, tile, MXU_DIM), lambda b, j: (b, j, 0)),
                   pl.BlockSpec((1, LANES, tile), lambda b, j: (b, 0, j))],
        out_shape=[jax.ShapeDtypeStruct((bsz, n, MXU_DIM), BF16),
                   jax.ShapeDtypeStruct((bsz, LANES, n), BF16)],
        compiler_params=pltpu.CompilerParams(vmem_limit_bytes=KV_VMEM_LIMIT_BYTES),
        name="shared_kv",
    )(tokens, mod, norm_g, wkv, kv_norm_g, tabs)


def _block_kernel(x_ref, xp_ref, xn_ref, mod_ref, tab_ref, ksh_ref, ckvt_ref, kshc_ref, ckvtc_ref,
                  ng_ref, qg_ref, bg_ref, ps_ref, fg_ref,
                  wu_ref, wgp_ref, wqa_ref, wqc_ref, wgm_ref, wmp_ref, wmm_ref,
                  wpool_ref, wv_ref, wpa_ref, wpb_ref, wout_ref,
                  o_ref, h_scr, qc_scr, ot_scr, l_scr, uext_scr, lvl_scr, s0_scr, s1_scr, *,
                  seq_len, exp2_scale):
    t = TOKEN_TILE
    i = pl.program_id(1)
    mod = mod_ref[0]
    shift, gate = mod[0:1], mod[2:3]
    gain = ng_ref[...] * (1.0 + mod[1:2])

    def mod_norm(v):
        return _rms(v) * gain + shift

    x = x_ref[0]
    h_main = mod_norm(x)
    h_halo = mod_norm(jnp.concatenate([xp_ref[0], xn_ref[0]], axis=0))
    hb = h_main.astype(BF16)
    h_scr[...] = hb

    qa = _dot(hb, wqa_ref[...])
    gate_mla = _silu(_dot(hb, wgm_ref[...]))
    q_all = _dot((_rms(qa) * qg_ref[...]).astype(BF16), wqc_ref[...]) * exp2_scale
    for hh in range(N_HEADS):
        c0 = hh * MXU_DIM
        qc_scr[hh] = jnp.concatenate([q_all[:, c0:c0 + LANES],
                                      q_all[:, c0 + LANES:c0 + MXU_DIM] * tab_ref[...]],
                                     axis=1).astype(BF16)

    keys = ksh_ref.shape[1] + kshc_ref.shape[1]
    n_lat_chunks = ksh_ref.shape[1] // MXU_DIM
    kc = MXU_DIM
    n_chunks = keys // kc

    def score_chunk(c, hh, s_scr, m8):
        rows = slice(c * kc, (c + 1) * kc)
        k_c = (ksh_ref[0, rows, :] if c < n_lat_chunks
               else kshc_ref[0, (c - n_lat_chunks) * kc:(c - n_lat_chunks + 1) * kc, :])
        sc = lax.dot_general(k_c, qc_scr[hh], (((1,), (1,)), ((), ())),
                             preferred_element_type=F32)
        s_scr[rows, :] = sc
        return jnp.maximum(m8, jnp.max(sc.reshape(kc // SUBLANES, SUBLANES, t), axis=0))

    def value_chunk(c, s_scr, m, l8, ot):
        rows = slice(c * kc, (c + 1) * kc)
        p = jnp.exp2(s_scr[rows, :] - m)
        l8 = l8 + jnp.sum(p.reshape(kc // SUBLANES, SUBLANES, t), axis=0)
        v_c = (ckvt_ref[0, :, rows] if c < n_lat_chunks
               else ckvtc_ref[0, :, (c - n_lat_chunks) * kc:(c - n_lat_chunks + 1) * kc])
        return l8, ot + _dot(v_c, p.astype(BF16))

    neg_inf8 = jnp.full((SUBLANES, t), -jnp.inf, F32)

    def head_step(hh, h1, h2, s_cur, s_oth, m_cur, m8_1):
        m8_2 = neg_inf8
        l8 = jnp.zeros((SUBLANES, t), F32)
        ot = jnp.zeros((LANES, t), F32)
        for c in range(n_chunks):
            ahead = c + SCORE_LEAD
            if ahead < n_chunks:
                if h1 is not None:
                    m8_1 = score_chunk(ahead, h1, s_oth, m8_1)
            elif h2 is not None:
                m8_2 = score_chunk(ahead - n_chunks, h2, s_cur, m8_2)
            l8, ot = value_chunk(c, s_cur, m_cur, l8, ot)
        ot_scr[hh] = ot
        l_scr[hh] = l8
        return jnp.max(m8_1, axis=0, keepdims=True), m8_2

    def head_trip(first, n_heads, carry):
        m_cur, m8_next = carry
        bufs = (s0_scr, s1_scr)
        for k in range(n_heads):
            hh = first + k
            if isinstance(first, int):
                h1 = hh + 1 if hh + 1 < N_HEADS else None
                h2 = hh + 2 if hh + 2 < N_HEADS else None
            else:
                h1, h2 = hh + 1, hh + 2
            m_cur, m8_next = head_step(hh, h1, h2, bufs[k % 2], bufs[(k + 1) % 2], m_cur, m8_next)
        return m_cur, m8_next

    m8_0, m8_1 = neg_inf8, neg_inf8
    for c in range(n_chunks):
        m8_0 = score_chunk(c, 0, s0_scr, m8_0)
    for c in range(SCORE_LEAD):
        m8_1 = score_chunk(c, 1, s1_scr, m8_1)
    n_trips = (N_HEADS - 1) // HEADS_PER_TRIP
    carry = lax.fori_loop(0, n_trips, lambda j, cr: head_trip(HEADS_PER_TRIP * j, HEADS_PER_TRIP, cr),
                          (jnp.max(m8_0, axis=0, keepdims=True), m8_1))
    head_trip(n_trips * HEADS_PER_TRIP, N_HEADS - n_trips * HEADS_PER_TRIP, carry)

    def latent_out(hh):
        inv_l = 1.0 / jnp.sum(l_scr[hh], axis=0, keepdims=True)
        return (ot_scr[hh] * inv_l).T

    attn = jnp.concatenate(
        [_dot(jnp.concatenate([latent_out(2 * j), latent_out(2 * j + 1)], axis=1).astype(BF16), wv_ref[j])
         for j in range(N_HEADS // 2)], axis=1)
    y_mla = _dot((attn * gate_mla).astype(BF16), wpb_ref[...])

    h_all = jnp.concatenate([h_main, h_halo], axis=0).astype(BF16)
    has_prev = (i > 0).astype(F32)
    has_next = (i < pl.num_programs(1) - 1).astype(F32)
    n_ext = t + 2 * POOL_HALO
    gw = uext_scr.shape[1] // len(POOL_WINDOWS)
    uext_scr[n_ext:n_ext + SUBLANES] = jnp.zeros((SUBLANES, uext_scr.shape[1]), F32)
    for lv in range(lvl_scr.shape[0]):
        lvl_scr[lv, n_ext:n_ext + SUBLANES] = jnp.zeros((SUBLANES, gw), F32)
    pos = i * t + lax.broadcasted_iota(jnp.int32, (t, LANES), 0)

    def group_inputs(g):
        cols = slice(g * gw, (g + 1) * gw)
        return _dot(h_all, wu_ref[:, cols]), _dot(h_scr[...], wgp_ref[:, cols])

    y_pool = jnp.zeros((t, wpa_ref.shape[1]), F32)
    nxt = group_inputs(0)
    for g, w in enumerate(POOL_WINDOWS):
        hw = w // 2
        cols = slice(g * gw, (g + 1) * gw)
        u_g, gp_g = nxt
        if g + 1 < len(POOL_WINDOWS):
            nxt = group_inputs(g + 1)
        uext_scr[0:POOL_HALO, cols] = u_g[t:t + POOL_HALO] * has_prev
        uext_scr[POOL_HALO:POOL_HALO + t, cols] = u_g[0:t]
        uext_scr[POOL_HALO + t:n_ext, cols] = u_g[t + POOL_HALO:n_ext] * has_next

        def rows_of(k, n, lv=None):
            return uext_scr[pl.ds(k, n), cols] if lv is None else lvl_scr[lv, pl.ds(k, n)]
        lv, span = None, 1
        while span < hw:
            nlv = 0 if lv is None else lv + 1
            lvl_scr[nlv, 0:n_ext] = rows_of(0, n_ext, lv) + rows_of(span, n_ext, lv)
            lv, span = nlv, 2 * span
        acc = rows_of(POOL_HALO - hw, t, lv) + rows_of(POOL_HALO, t, lv)
        cnt = (jnp.minimum(pos + hw, seq_len) - jnp.maximum(pos - hw, 0)).astype(F32)
        inv_cnt = jnp.concatenate([1.0 / cnt] * (gw // LANES), axis=1)
        d = acc * inv_cnt - uext_scr[pl.ds(POOL_HALO, t), cols]
        mixed = _dot(d.astype(BF16), wpool_ref[g]) * ps_ref[:, cols]
        y_pool = y_pool + _dot((mixed * _silu(gp_g)).astype(BF16), wpa_ref[cols, :])

    g_pool = _sigmoid(_dot(h_scr[...], wmp_ref[...]) + bg_ref[0:1])
    g_mla = _sigmoid(_dot(h_scr[...], wmm_ref[...]) + bg_ref[1:2])
    z = (g_pool * y_pool + g_mla * y_mla).astype(BF16)
    out = x + gate * _dot(z, wout_ref[...])
    o_ref[0] = _rms(out) * fg_ref[...]


def _const_spec(shape):
    zeros = (0,) * len(shape)
    return pl.BlockSpec(shape, lambda b, i: zeros, pipeline_mode=pl.Buffered(1))


def _fused_block(x, mod, tab_q, ksh, ckvt, ksh_ctx, ckvt_ctx, vecs, weights):
    bsz, seq, d = x.shape
    t = TOKEN_TILE
    n_lat, n_ctx = ksh.shape[1], ksh_ctx.shape[1] // bsz
    keys = n_lat + n_ctx
    halo_blocks = t // POOL_HALO
    last_halo = seq // POOL_HALO - 1
    in_specs = [
        pl.BlockSpec((1, t, d), lambda b, i: (b, i, 0)),
        pl.BlockSpec((1, POOL_HALO, d), lambda b, i: (b, jnp.maximum(i * halo_blocks - 1, 0), 0)),
        pl.BlockSpec((1, POOL_HALO, d), lambda b, i: (b, jnp.minimum((i + 1) * halo_blocks, last_halo), 0)),
        pl.BlockSpec((1, 3, d), lambda b, i: (b, 0, 0)),
        pl.BlockSpec((t, LANES), lambda b, i: (i, 0)),
        pl.BlockSpec((1, n_lat, MXU_DIM), lambda b, i: (b, 0, 0)),
        pl.BlockSpec((1, LANES, n_lat), lambda b, i: (b, 0, 0)),
        pl.BlockSpec((1, n_ctx, MXU_DIM), lambda b, i: (0, b, 0)),
        pl.BlockSpec((1, LANES, n_ctx), lambda b, i: (0, 0, b)),
    ]
    in_specs += [_const_spec(v.shape) for v in vecs]
    in_specs += [_const_spec(w.shape) for w in weights]
    exp2_scale = float((QK_NOPE + QK_ROPE) ** -0.5 * math.log2(math.e))
    return pl.pallas_call(
        functools.partial(_block_kernel, seq_len=seq, exp2_scale=exp2_scale),
        grid=(bsz, seq // t),
        in_specs=in_specs,
        out_specs=pl.BlockSpec((1, t, d), lambda b, i: (b, i, 0)),
        out_shape=jax.ShapeDtypeStruct((bsz, seq, d), F32),
        scratch_shapes=[pltpu.VMEM((t, d), BF16),
                        pltpu.VMEM((N_HEADS, t, MXU_DIM), BF16),
                        pltpu.VMEM((N_HEADS, LANES, t), F32),
                        pltpu.VMEM((N_HEADS, SUBLANES, t), F32),
                        pltpu.VMEM((t + 2 * POOL_HALO + SUBLANES, d), F32),
                        pltpu.VMEM((len(POOL_WINDOWS) - 1, t + 2 * POOL_HALO + SUBLANES,
                                    d // len(POOL_WINDOWS)), F32),
                        pltpu.VMEM((keys, t), F32),
                        pltpu.VMEM((keys, t), F32)],
        compiler_params=pltpu.CompilerParams(
            dimension_semantics=("arbitrary", "arbitrary"),
            vmem_limit_bytes=VMEM_LIMIT_BYTES),
        name="fused_block",
    )(x, x, x, mod, tab_q, ksh, ckvt, ksh_ctx, ckvt_ctx, *vecs, *weights)


def _rope_tables(n_tokens, n_plain):
    f32 = np.float32
    t = np.arange(n_tokens)
    t_row = (t // GRID_W).astype(f32)
    t_col = (t % GRID_W).astype(f32)
    half = QK_ROPE // 2
    inv = (f32(1.0) / (f32(ROPE_THETA) ** (np.arange(0, half, 2, dtype=f32) / f32(half)))).astype(f32)
    ang_r = t_row[:, None] * inv
    ang_c = t_col[:, None] * inv
    ang = np.concatenate([ang_r, ang_r, ang_c, ang_c], axis=-1)
    cos, sin = np.cos(ang).astype(f32), np.sin(ang).astype(f32)
    quarter = QK_ROPE // 4
    first = (np.arange(QK_ROPE) % (2 * quarter)) < quarter
    sin_a = np.where(first, -sin, f32(0.0))
    sin_b = np.where(first, f32(0.0), sin)
    tabs = np.stack([cos, sin_a, sin_b])
    plain = np.stack([np.ones((n_plain, QK_ROPE), f32),
                      np.zeros((n_plain, QK_ROPE), f32),
                      np.zeros((n_plain, QK_ROPE), f32)])
    tabs = np.concatenate([tabs, plain], axis=1)
    return np.pad(tabs, ((0, 0), (0, 0), (0, LANES - QK_ROPE))), np.concatenate([cos, sin], axis=-1)


def kernel(x, c, ctx, c_ctx, ada_w, ada_b, norm_g, w_in, b_gate, w_pool, pool_scale, q_norm_g, w_qb,
           kv_norm_g, w_kvb, w_proj_pool, w_proj_mla, w_out, final_g):
    assert ada_w.shape[0] == 1, "single-layer block"
    bsz, seq, d = x.shape
    ctx_len = ctx.shape[1]
    q_lora = q_norm_g.shape[1]
    kv_lora = kv_norm_g.shape[1]
    pool_w = pool_scale.shape[1]
    mla_w = w_proj_mla.shape[1]
    assert kv_lora == LANES and QK_NOPE == LANES and V_DIM == LANES and 2 * QK_ROPE == LANES

    cond = jnp.concatenate([c, c_ctx[None, :]], axis=0)
    mod = _ada_params(cond, ada_w[0], ada_b[0]).reshape(bsz + 1, 3, d)

    splits = (pool_w, pool_w, q_lora, kv_lora + QK_ROPE, mla_w, d, d)
    offs = [0]
    for s in splits:
        offs.append(offs[-1] + s)
    w_in0 = w_in[0]
    wu, wgp, wqa, wkva, wgm, wmp, wmm = [w_in0[:, offs[k]:offs[k + 1]] for k in range(7)]
    wkv_pad = jnp.pad(wkva, ((0, 0), (0, MXU_DIM - wkva.shape[1]))).astype(BF16)

    wqb3 = w_qb[0].reshape(q_lora, N_HEADS, QK_NOPE + QK_ROPE)
    wkvb3 = w_kvb[0].reshape(kv_lora, N_HEADS, QK_NOPE + V_DIM)
    wk_t = jnp.transpose(wkvb3[:, :, :QK_NOPE], (1, 2, 0))
    wv = jnp.transpose(wkvb3[:, :, QK_NOPE:], (1, 0, 2))
    wq_abs = _fold_key_proj(wqb3[:, :, :QK_NOPE].reshape(q_lora, N_HEADS * QK_NOPE), wk_t)
    lane = jnp.arange(QK_ROPE)
    first = (lane % (QK_ROPE // 2)) < QK_ROPE // 4
    partner = jnp.where(first, lane + QK_ROPE // 4, lane - QK_ROPE // 4)
    sign = jnp.where(first, -1.0, 1.0).astype(F32)
    wq_rope = wqb3[:, :, QK_NOPE:]
    wq_cat = jnp.concatenate([wq_abs.reshape(q_lora, N_HEADS, kv_lora),
                              wq_rope, wq_rope[:, :, partner] * sign], axis=2)
    wq_cat = wq_cat.reshape(q_lora, N_HEADS * MXU_DIM).astype(BF16)

    wv2 = wv.reshape(N_HEADS // 2, 2, kv_lora, V_DIM)
    zblk = jnp.zeros_like(wv2[:, 0])
    wv_bd = jnp.concatenate([jnp.concatenate([wv2[:, 0], zblk], axis=-1),
                             jnp.concatenate([zblk, wv2[:, 1]], axis=-1)], axis=1).astype(BF16)

    tabs, tab_q = _rope_tables(seq, ctx_len)
    ksh, ckvt = _shared_kv(x, mod, None, norm_g, wkv_pad, kv_norm_g, tabs[:, :seq], KV_TILE)
    ctx_flat = ctx.reshape(1, bsz * ctx_len, d)
    ctx_tile = math.gcd(bsz * ctx_len, KV_TILE)
    assert ctx_tile % ctx_len == 0
    tabs_ctx = np.tile(tabs[:, seq:], (1, ctx_tile // ctx_len, 1))
    ksh_ctx, ckvt_ctx = _shared_kv(ctx_flat, mod, bsz, norm_g, wkv_pad, kv_norm_g, tabs_ctx, ctx_tile)

    vecs = (norm_g, q_norm_g, b_gate[0].reshape(2, d), pool_scale, final_g.reshape(1, d))
    weights = (wu.astype(BF16), wgp.astype(BF16), wqa.astype(BF16), wq_cat, wgm.astype(BF16),
               wmp.astype(BF16), wmm.astype(BF16), w_pool[0].astype(BF16), wv_bd,
               w_proj_pool[0].astype(BF16), w_proj_mla[0].astype(BF16), w_out[0].astype(BF16))
    return _fused_block(x, mod, tab_q, ksh, ckvt, ksh_ctx, ckvt_ctx, vecs, weights)
```

```python
import functools
import math

import jax
import jax.numpy as jnp
import numpy as np
from jax import lax
from jax.experimental import pallas as pl
from jax.experimental.pallas import tpu as pltpu

GRID_W = 64
N_HEADS = 16
QK_NOPE = 128
QK_ROPE = 64
V_DIM = 128
POOL_WINDOWS = (2, 4, 8, 16)
ROPE_THETA = 10000.0
NORM_EPS = 1e-6

LANES = 128
SUBLANES = 8
MXU_DIM = 256
VMEM_LIMIT_BYTES = 56 * 1024 * 1024

TOKEN_TILE = 256
KV_TILE = 1024
KV_VMEM_LIMIT_BYTES = 40 * 1024 * 1024
KV_ROW_CHUNK = 256
FOLD_HEADS_PER_STEP = 4
POOL_HALO = max(POOL_WINDOWS) // 2
SCORE_LEAD = 2
HEADS_PER_TRIP = 6

F32 = jnp.float32
BF16 = jnp.bfloat16


def _sigmoid(v):
    return 0.5 + 0.5 * jnp.tanh(0.5 * v)


def _silu(v):
    hv = 0.5 * v
    return hv + hv * jnp.tanh(hv)


def _rms(v):
    return v * lax.rsqrt(jnp.mean(v * v, axis=-1, keepdims=True) + NORM_EPS)


def _dot(a, b):
    return jnp.dot(a, b, preferred_element_type=F32)


def _rope(v, tab_ref):
    return (v * tab_ref[0]
            + pltpu.roll(v, LANES - QK_ROPE // 4, 1) * tab_ref[1]
            + pltpu.roll(v, QK_ROPE // 4, 1) * tab_ref[2])


def _ada_kernel(cond_ref, w_ref, b_ref, o_ref):
    s = _silu(cond_ref[...]).astype(BF16)
    o_ref[...] = _dot(s, w_ref[...].astype(BF16)) + b_ref[...]


def _ada_params(cond, w, b):
    rows, d = cond.shape
    n = w.shape[1]
    bn = d
    return pl.pallas_call(
        _ada_kernel,
        grid=(n // bn,),
        in_specs=[pl.BlockSpec((rows, d), lambda j: (0, 0)),
                  pl.BlockSpec((d, bn), lambda j: (0, j)),
                  pl.BlockSpec((1, bn), lambda j: (0, j))],
        out_specs=pl.BlockSpec((rows, bn), lambda j: (0, j)),
        out_shape=jax.ShapeDtypeStruct((rows, n), F32),
        name="ada_params",
    )(cond, w, b.reshape(1, n))


def _qabs_kernel(wq_ref, wkt_ref, o_ref):
    for k in range(FOLD_HEADS_PER_STEP):
        o_ref[:, k * LANES:(k + 1) * LANES] = jnp.dot(
            wq_ref[:, k * QK_NOPE:(k + 1) * QK_NOPE], wkt_ref[k],
            precision=lax.Precision.HIGHEST, preferred_element_type=F32)


def _fold_key_proj(wq_nope, wk_t):
    q_lora = wq_nope.shape[0]
    kv_lora = wk_t.shape[2]
    hs = FOLD_HEADS_PER_STEP
    return pl.pallas_call(
        _qabs_kernel,
        grid=(N_HEADS // hs,),
        in_specs=[pl.BlockSpec((q_lora, hs * QK_NOPE), lambda g: (0, g)),
                  pl.BlockSpec((hs, QK_NOPE, kv_lora), lambda g: (g, 0, 0))],
        out_specs=pl.BlockSpec((q_lora, hs * kv_lora), lambda g: (0, g)),
        out_shape=jax.ShapeDtypeStruct((q_lora, N_HEADS * kv_lora), F32),
        name="fold_key_proj",
    )(wq_nope, wk_t)


def _kv_kernel(x_ref, mod_ref, ng_ref, wkv_ref, kvg_ref, tab_ref, ksh_ref, ckvt_ref):
    mod = mod_ref[0]
    gain, shift = ng_ref[...] * (1.0 + mod[1:2]), mod[0:1]
    rows_total = x_ref.shape[1]
    step = min(rows_total, KV_ROW_CHUNK)
    chunks = [slice(r0, r0 + step) for r0 in range(0, rows_total, step)]

    def normed(rows):
        return (_rms(x_ref[0, rows, :]) * gain + shift).astype(BF16)

    h = normed(chunks[0])
    for k, rows in enumerate(chunks):
        h_next = normed(chunks[k + 1]) if k + 1 < len(chunks) else None
        kv = _dot(h, wkv_ref[...])
        ckvn = _rms(kv[:, :LANES]) * kvg_ref[...]
        kr = _rope(kv[:, LANES:], tab_ref.at[:, rows, :])
        kr = kr + pltpu.roll(kr, QK_ROPE, 1)
        ksh_ref[0, rows, :] = jnp.concatenate([ckvn, kr], axis=1).astype(BF16)
        ckvt_ref[0, :, rows] = ckvn.T.astype(BF16)
        h = h_next


def _shared_kv(tokens, mod, mod_row, norm_g, wkv, kv_norm_g, tabs, tile):
    bsz, n, d = tokens.shape
    assert n % tile == 0 and tabs.shape[1] in (n, tile)
    mod_idx = (lambda b, j: (b, 0, 0)) if mod_row is None else (lambda b, j: (mod_row, 0, 0))
    tab_idx = (lambda b, j: (0, j, 0)) if tabs.shape[1] == n else (lambda b, j: (0, 0, 0))
    return pl.pallas_call(
        _kv_kernel,
        grid=(bsz, n // tile),
        in_specs=[pl.BlockSpec((1, tile, d), lambda b, j: (b, j, 0)),
                  pl.BlockSpec((1, 3, d), mod_idx),
                  pl.BlockSpec((1, d), lambda b, j: (0, 0)),
                  pl.BlockSpec((d, MXU_DIM), lambda b, j: (0, 0)),
                  pl.BlockSpec((1, LANES), lambda b, j: (0, 0)),
                  pl.BlockSpec((3, tile, LANES), tab_idx)],
        out_specs=[pl.BlockSpec((1, tile, MXU_DIM), lambda b, j: (b, j, 0)),
                   pl.BlockSpec((1, LANES, tile), lambda b, j: (b, 0, j))],
        out_shape=[jax.ShapeDtypeStruct((bsz, n, MXU_DIM), BF16),
                   jax.ShapeDtypeStruct((bsz, LANES, n), BF16)],
        compiler_params=pltpu.CompilerParams(vmem_limit_bytes=KV_VMEM_LIMIT_BYTES),
        name="shared_kv",
    )(tokens, mod, norm_g, wkv, kv_norm_g, tabs)


def _block_kernel(x_ref, xp_ref, xn_ref, mod_ref, tab_ref, ksh_ref, ckvt_ref, kshc_ref, ckvtc_ref,
                  ng_ref, qg_ref, bg_ref, ps_ref, fg_ref,
                  win_ref, wqc_ref, wpool_ref, wv_ref, wpa_ref, wpb_ref, wout_ref,
                  o_ref, h_scr, qc_scr, ot_scr, l_scr, uext_scr, lvl_scr, s0_scr, s1_scr, *,
                  seq_len, exp2_scale, in_cols):
    t = TOKEN_TILE
    c_u, c_gp, c_qa, c_gm, c_mp, c_mm, c_end = in_cols
    i = pl.program_id(1)
    mod = mod_ref[0]
    shift, gate = mod[0:1], mod[2:3]
    gain = ng_ref[...] * (1.0 + mod[1:2])

    def mod_norm(v):
        return _rms(v) * gain + shift

    x = x_ref[0]
    h_main = mod_norm(x)
    h_halo = mod_norm(jnp.concatenate([xp_ref[0], xn_ref[0]], axis=0))
    hb = h_main.astype(BF16)
    h_scr[...] = hb

    qa = _dot(hb, win_ref[:, c_qa:c_gm])
    gate_mla = _silu(_dot(hb, win_ref[:, c_gm:c_mp]))
    q_all = _dot((_rms(qa) * qg_ref[...]).astype(BF16), wqc_ref[...]) * exp2_scale
    for hh in range(N_HEADS):
        c0 = hh * MXU_DIM
        qc_scr[hh] = jnp.concatenate([q_all[:, c0:c0 + LANES],
                                      q_all[:, c0 + LANES:c0 + MXU_DIM] * tab_ref[...]],
                                     axis=1).astype(BF16)

    keys = ksh_ref.shape[1] + kshc_ref.shape[1]
    n_lat_chunks = ksh_ref.shape[1] // MXU_DIM
    kc = MXU_DIM
    n_chunks = keys // kc

    def score_chunk(c, hh, s_scr, m8):
        rows = slice(c * kc, (c + 1) * kc)
        k_c = (ksh_ref[0, rows, :] if c < n_lat_chunks
               else kshc_ref[0, (c - n_lat_chunks) * kc:(c - n_lat_chunks + 1) * kc, :])
        sc = lax.dot_general(k_c, qc_scr[hh], (((1,), (1,)), ((), ())),
                             preferred_element_type=F32)
        s_scr[rows, :] = sc
        return jnp.maximum(m8, jnp.max(sc.reshape(kc // SUBLANES, SUBLANES, t), axis=0))

    def value_chunk(c, s_scr, m, l8, ot):
        rows = slice(c * kc, (c + 1) * kc)
        p = jnp.exp2(s_scr[rows, :] - m)
        l8 = l8 + jnp.sum(p.reshape(kc // SUBLANES, SUBLANES, t), axis=0)
        v_c = (ckvt_ref[0, :, rows] if c < n_lat_chunks
               else ckvtc_ref[0, :, (c - n_lat_chunks) * kc:(c - n_lat_chunks + 1) * kc])
        return l8, ot + _dot(v_c, p.astype(BF16))

    neg_inf8 = jnp.full((SUBLANES, t), -jnp.inf, F32)

    def head_step(hh, h1, h2, s_cur, s_oth, m_cur, m8_1):
        m8_2 = neg_inf8
        l8 = jnp.zeros((SUBLANES, t), F32)
        ot = jnp.zeros((LANES, t), F32)
        for c in range(n_chunks):
            ahead = c + SCORE_LEAD
            if ahead < n_chunks:
                if h1 is not None:
                    m8_1 = score_chunk(ahead, h1, s_oth, m8_1)
            elif h2 is not None:
                m8_2 = score_chunk(ahead - n_chunks, h2, s_cur, m8_2)
            l8, ot = value_chunk(c, s_cur, m_cur, l8, ot)
        ot_scr[hh] = ot
        l_scr[hh] = l8
        return jnp.max(m8_1, axis=0, keepdims=True), m8_2

    def head_trip(first, n_heads, carry):
        m_cur, m8_next = carry
        bufs = (s0_scr, s1_scr)
        for k in range(n_heads):
            hh = first + k
            if isinstance(first, int):
                h1 = hh + 1 if hh + 1 < N_HEADS else None
                h2 = hh + 2 if hh + 2 < N_HEADS else None
            else:
                h1, h2 = hh + 1, hh + 2
            m_cur, m8_next = head_step(hh, h1, h2, bufs[k % 2], bufs[(k + 1) % 2], m_cur, m8_next)
        return m_cur, m8_next

    m8_0, m8_1 = neg_inf8, neg_inf8
    for c in range(n_chunks):
        m8_0 = score_chunk(c, 0, s0_scr, m8_0)
    for c in range(SCORE_LEAD):
        m8_1 = score_chunk(c, 1, s1_scr, m8_1)
    n_trips = (N_HEADS - 1) // HEADS_PER_TRIP
    carry = lax.fori_loop(0, n_trips, lambda j, cr: head_trip(HEADS_PER_TRIP * j, HEADS_PER_TRIP, cr),
                          (jnp.max(m8_0, axis=0, keepdims=True), m8_1))
    head_trip(n_trips * HEADS_PER_TRIP, N_HEADS - n_trips * HEADS_PER_TRIP, carry)

    def latent_out(hh):
        inv_l = 1.0 / jnp.sum(l_scr[hh], axis=0, keepdims=True)
        return (ot_scr[hh] * inv_l).T

    attn = jnp.concatenate(
        [_dot(jnp.concatenate([latent_out(2 * j), latent_out(2 * j + 1)], axis=1).astype(BF16), wv_ref[j])
         for j in range(N_HEADS // 2)], axis=1)
    y_mla = _dot((attn * gate_mla).astype(BF16), wpb_ref[...])

    h_all = jnp.concatenate([h_main, h_halo], axis=0).astype(BF16)
    has_prev = (i > 0).astype(F32)
    has_next = (i < pl.num_programs(1) - 1).astype(F32)
    n_ext = t + 2 * POOL_HALO
    gw = uext_scr.shape[1] // len(POOL_WINDOWS)
    uext_scr[n_ext:n_ext + SUBLANES] = jnp.zeros((SUBLANES, uext_scr.shape[1]), F32)
    for lv in range(lvl_scr.shape[0]):
        lvl_scr[lv, n_ext:n_ext + SUBLANES] = jnp.zeros((SUBLANES, gw), F32)
    pos = i * t + lax.broadcasted_iota(jnp.int32, (t, LANES), 0)

    def group_inputs(g):
        return (_dot(h_all, win_ref[:, c_u + g * gw:c_u + (g + 1) * gw]),
                _dot(h_scr[...], win_ref[:, c_gp + g * gw:c_gp + (g + 1) * gw]))

    y_pool = jnp.zeros((t, wpa_ref.shape[1]), F32)
    nxt = group_inputs(0)
    for g, w in enumerate(POOL_WINDOWS):
        hw = w // 2
        cols = slice(g * gw, (g + 1) * gw)
        u_g, gp_g = nxt
        if g + 1 < len(POOL_WINDOWS):
            nxt = group_inputs(g + 1)
        uext_scr[0:POOL_HALO, cols] = u_g[t:t + POOL_HALO] * has_prev
        uext_scr[POOL_HALO:POOL_HALO + t, cols] = u_g[0:t]
        uext_scr[POOL_HALO + t:n_ext, cols] = u_g[t + POOL_HALO:n_ext] * has_next

        def rows_of(k, n, lv=None):
            return uext_scr[pl.ds(k, n), cols] if lv is None else lvl_scr[lv, pl.ds(k, n)]
        lv, span = None, 1
        while span < hw:
            nlv = 0 if lv is None else lv + 1
            lvl_scr[nlv, 0:n_ext] = rows_of(0, n_ext, lv) + rows_of(span, n_ext, lv)
            lv, span = nlv, 2 * span
        acc = rows_of(POOL_HALO - hw, t, lv) + rows_of(POOL_HALO, t, lv)
        cnt = (jnp.minimum(pos + hw, seq_len) - jnp.maximum(pos - hw, 0)).astype(F32)
        inv_cnt = jnp.concatenate([1.0 / cnt] * (gw // LANES), axis=1)
        d = acc * inv_cnt - uext_scr[pl.ds(POOL_HALO, t), cols]
        mixed = _dot(d.astype(BF16), wpool_ref[g]) * ps_ref[:, cols]
        y_pool = y_pool + _dot((mixed * _silu(gp_g)).astype(BF16), wpa_ref[cols, :])

    g_pool = _sigmoid(_dot(h_scr[...], win_ref[:, c_mp:c_mm]) + bg_ref[0:1])
    g_mla = _sigmoid(_dot(h_scr[...], win_ref[:, c_mm:c_end]) + bg_ref[1:2])
    z = (g_pool * y_pool + g_mla * y_mla).astype(BF16)
    out = x + gate * _dot(z, wout_ref[...])
    o_ref[0] = _rms(out) * fg_ref[...]


def _const_spec(shape):
    zeros = (0,) * len(shape)
    return pl.BlockSpec(shape, lambda b, i: zeros, pipeline_mode=pl.Buffered(1))


def _fused_block(x, mod, tab_q, ksh, ckvt, ksh_ctx, ckvt_ctx, vecs, weights, in_cols):
    bsz, seq, d = x.shape
    t = TOKEN_TILE
    n_lat, n_ctx = ksh.shape[1], ksh_ctx.shape[1] // bsz
    keys = n_lat + n_ctx
    halo_blocks = t // POOL_HALO
    last_halo = seq // POOL_HALO - 1
    in_specs = [
        pl.BlockSpec((1, t, d), lambda b, i: (b, i, 0)),
        pl.BlockSpec((1, POOL_HALO, d), lambda b, i: (b, jnp.maximum(i * halo_blocks - 1, 0), 0)),
        pl.BlockSpec((1, POOL_HALO, d), lambda b, i: (b, jnp.minimum((i + 1) * halo_blocks, last_halo), 0)),
        pl.BlockSpec((1, 3, d), lambda b, i: (b, 0, 0)),
        pl.BlockSpec((t, LANES), lambda b, i: (i, 0)),
        pl.BlockSpec((1, n_lat, MXU_DIM), lambda b, i: (b, 0, 0)),
        pl.BlockSpec((1, LANES, n_lat), lambda b, i: (b, 0, 0)),
        pl.BlockSpec((1, n_ctx, MXU_DIM), lambda b, i: (0, b, 0)),
        pl.BlockSpec((1, LANES, n_ctx), lambda b, i: (0, 0, b)),
    ]
    in_specs += [_const_spec(v.shape) for v in vecs]
    in_specs += [_const_spec(w.shape) for w in weights]
    exp2_scale = float((QK_NOPE + QK_ROPE) ** -0.5 * math.log2(math.e))
    return pl.pallas_call(
        functools.partial(_block_kernel, seq_len=seq, exp2_scale=exp2_scale, in_cols=in_cols),
        grid=(bsz, seq // t),
        in_specs=in_specs,
        out_specs=pl.BlockSpec((1, t, d), lambda b, i: (b, i, 0)),
        out_shape=jax.ShapeDtypeStruct((bsz, seq, d), F32),
        scratch_shapes=[pltpu.VMEM((t, d), BF16),
                        pltpu.VMEM((N_HEADS, t, MXU_DIM), BF16),
                        pltpu.VMEM((N_HEADS, LANES, t), F32),
                        pltpu.VMEM((N_HEADS, SUBLANES, t), F32),
                        pltpu.VMEM((t + 2 * POOL_HALO + SUBLANES, d), F32),
                        pltpu.VMEM((len(POOL_WINDOWS) - 1, t + 2 * POOL_HALO + SUBLANES,
                                    d // len(POOL_WINDOWS)), F32),
                        pltpu.VMEM((keys, t), F32),
                        pltpu.VMEM((keys, t), F32)],
        compiler_params=pltpu.CompilerParams(
            dimension_semantics=("arbitrary", "arbitrary"),
            vmem_limit_bytes=VMEM_LIMIT_BYTES),
        name="fused_block",
    )(x, x, x, mod, tab_q, ksh, ckvt, ksh_ctx, ckvt_ctx, *vecs, *weights)


def _rope_tables(n_tokens, n_plain):
    f32 = np.float32
    t = np.arange(n_tokens)
    t_row = (t // GRID_W).astype(f32)
    t_col = (t % GRID_W).astype(f32)
    half = QK_ROPE // 2
    inv = (f32(1.0) / (f32(ROPE_THETA) ** (np.arange(0, half, 2, dtype=f32) / f32(half)))).astype(f32)
    ang_r = t_row[:, None] * inv
    ang_c = t_col[:, None] * inv
    ang = np.concatenate([ang_r, ang_r, ang_c, ang_c], axis=-1)
    cos, sin = np.cos(ang).astype(f32), np.sin(ang).astype(f32)
    quarter = QK_ROPE // 4
    first = (np.arange(QK_ROPE) % (2 * quarter)) < quarter
    sin_a = np.where(first, -sin, f32(0.0))
    sin_b = np.where(first, f32(0.0), sin)
    tabs = np.stack([cos, sin_a, sin_b])
    plain = np.stack([np.ones((n_plain, QK_ROPE), f32),
                      np.zeros((n_plain, QK_ROPE), f32),
                      np.zeros((n_plain, QK_ROPE), f32)])
    tabs = np.concatenate([tabs, plain], axis=1)
    return np.pad(tabs, ((0, 0), (0, 0), (0, LANES - QK_ROPE))), np.concatenate([cos, sin], axis=-1)


def kernel(x, c, ctx, c_ctx, ada_w, ada_b, norm_g, w_in, b_gate, w_pool, pool_scale, q_norm_g, w_qb,
           kv_norm_g, w_kvb, w_proj_pool, w_proj_mla, w_out, final_g):
    assert ada_w.shape[0] == 1, "single-layer block"
    bsz, seq, d = x.shape
    ctx_len = ctx.shape[1]
    q_lora = q_norm_g.shape[1]
    kv_lora = kv_norm_g.shape[1]
    pool_w = pool_scale.shape[1]
    mla_w = w_proj_mla.shape[1]
    assert kv_lora == LANES and QK_NOPE == LANES and V_DIM == LANES and 2 * QK_ROPE == LANES

    cond = jnp.concatenate([c, c_ctx[None, :]], axis=0)
    mod = _ada_params(cond, ada_w[0], ada_b[0]).reshape(bsz + 1, 3, d)

    splits = (pool_w, pool_w, q_lora, kv_lora + QK_ROPE, mla_w, d, d)
    offs = [0]
    for s in splits:
        offs.append(offs[-1] + s)
    w_in0 = w_in[0]
    wkva = w_in0[:, offs[3]:offs[4]]
    wkv_pad = jnp.pad(wkva, ((0, 0), (0, MXU_DIM - wkva.shape[1]))).astype(BF16)

    wqb3 = w_qb[0].reshape(q_lora, N_HEADS, QK_NOPE + QK_ROPE)
    wkvb3 = w_kvb[0].reshape(kv_lora, N_HEADS, QK_NOPE + V_DIM)
    wk_t = jnp.transpose(wkvb3[:, :, :QK_NOPE], (1, 2, 0))
    wv = jnp.transpose(wkvb3[:, :, QK_NOPE:], (1, 0, 2))
    wq_abs = _fold_key_proj(wqb3[:, :, :QK_NOPE].reshape(q_lora, N_HEADS * QK_NOPE), wk_t)
    lane = jnp.arange(QK_ROPE)
    first = (lane % (QK_ROPE // 2)) < QK_ROPE // 4
    partner = jnp.where(first, lane + QK_ROPE // 4, lane - QK_ROPE // 4)
    sign = jnp.where(first, -1.0, 1.0).astype(F32)
    wq_rope = wqb3[:, :, QK_NOPE:]
    wq_cat = jnp.concatenate([wq_abs.reshape(q_lora, N_HEADS, kv_lora),
                              wq_rope, wq_rope[:, :, partner] * sign], axis=2)
    wq_cat = wq_cat.reshape(q_lora, N_HEADS * MXU_DIM).astype(BF16)

    wv2 = wv.reshape(N_HEADS // 2, 2, kv_lora, V_DIM)
    zblk = jnp.zeros_like(wv2[:, 0])
    wv_bd = jnp.concatenate([jnp.concatenate([wv2[:, 0], zblk], axis=-1),
                             jnp.concatenate([zblk, wv2[:, 1]], axis=-1)], axis=1).astype(BF16)

    tabs, tab_q = _rope_tables(seq, ctx_len)
    ksh, ckvt = _shared_kv(x, mod, None, norm_g, wkv_pad, kv_norm_g, tabs[:, :seq], KV_TILE)
    ctx_flat = ctx.reshape(1, bsz * ctx_len, d)
    ctx_tile = math.gcd(bsz * ctx_len, KV_TILE)
    assert ctx_tile % ctx_len == 0
    tabs_ctx = np.tile(tabs[:, seq:], (1, ctx_tile // ctx_len, 1))
    ksh_ctx, ckvt_ctx = _shared_kv(ctx_flat, mod, bsz, norm_g, wkv_pad, kv_norm_g, tabs_ctx, ctx_tile)

    vecs = (norm_g, q_norm_g, b_gate[0].reshape(2, d), pool_scale, final_g.reshape(1, d))
    w_blk = jnp.concatenate([w_in0[:, :offs[3]], w_in0[:, offs[4]:]], axis=1).astype(BF16)
    in_cols = tuple(offs[:4]) + tuple(o - splits[3] for o in offs[5:])
    weights = (w_blk, wq_cat, w_pool[0].astype(BF16), wv_bd,
               w_proj_pool[0].astype(BF16), w_proj_mla[0].astype(BF16), w_out[0].astype(BF16))
    return _fused_block(x, mod, tab_q, ksh, ckvt, ksh_ctx, ckvt_ctx, vecs, weights, in_cols)
```
